```python
import math
import jax, jax.numpy as jnp
from jax import lax
import numpy as np

D_MODEL = 1024
BATCH = 32
SEQ = 2048
DEPTH = 4

GRID_W = 64
ROPE_THETA = 500000.0
Q_BLOCK = 128

A_HEADS = 4
A_HEAD_DIM = 64
A_ROT = A_HEAD_DIM // 4
A_Q = A_HEADS * 2 * A_HEAD_DIM
A_K = A_HEADS * 2 * A_HEAD_DIM
A_V = A_HEADS * 2 * A_HEAD_DIM

B_HEADS = 8
B_NOPE = 64
B_ROPE = 32
B_V = 64
B_Q_RANK = 256
B_KV_RANK = 128

C_HEADS = 16
C_HEAD_DIM = 64
C_WIDTH = C_HEADS * C_HEAD_DIM
NA_ROWS = 8
NA_COLS = 16

N_GROUPS = 4
EXPERTS_PER_GROUP = 8
N_EXPERTS = N_GROUPS * EXPERTS_PER_GROUP
TOP_K = 2
D_EXPERT = 512
MOE_BLOCK = 128

DN_ALPHA = (2 * DEPTH) ** 0.25
DN_BETA = (8 * DEPTH) ** -0.25
LN_EPS = 1e-5
RMS_EPS = 1e-6

EVEN_IN = A_Q + A_K + A_V + B_Q_RANK + B_KV_RANK + B_ROPE
EVEN_SPLITS = (A_Q, A_Q + A_K, A_Q + A_K + A_V, A_Q + A_K + A_V + B_Q_RANK,
               A_Q + A_K + A_V + B_Q_RANK + B_KV_RANK)
EVEN_OUT = A_HEADS * 2 * A_HEAD_DIM + B_HEADS * B_V
N_EVEN = (DEPTH + 1) // 2
N_ODD = DEPTH // 2

kernel_name = "hybrid_diffattn_mla_natten_hmoe_encoder"


def layer_norm(x, g, b):
    xf = x.astype(jnp.float32)
    mu = jnp.mean(xf, axis=-1, keepdims=True)
    var = jnp.mean(jnp.square(xf - mu), axis=-1, keepdims=True)
    y = (xf - mu) * lax.rsqrt(var + LN_EPS) * g.astype(jnp.float32) + b.astype(jnp.float32)
    return y.astype(x.dtype)


def rms_norm(x, g):
    xf = x.astype(jnp.float32)
    y = xf * lax.rsqrt(jnp.mean(xf * xf, axis=-1, keepdims=True) + RMS_EPS) * g.astype(jnp.float32)
    return y.astype(x.dtype)


def rope_tables(seq, rot_dim):
    inv_freq = ROPE_THETA ** (-jnp.arange(0, rot_dim, 2, dtype=jnp.float32) / rot_dim)
    ang = jnp.arange(seq, dtype=jnp.float32)[:, None] * inv_freq[None, :]
    return jnp.cos(ang), jnp.sin(ang)


def apply_rope(x, cos, sin):
    half = cos.shape[-1]
    r = 2 * half
    cos = cos.astype(x.dtype)
    sin = sin.astype(x.dtype)
    x1 = x[..., :half]
    x2 = x[..., half:r]
    return jnp.concatenate([x1 * cos - x2 * sin, x2 * cos + x1 * sin, x[..., r:]], axis=-1)


def diff_attention(q, k, v, lam, subln_g, lambda_init, cos, sin):
    B, S, H, _, d = q.shape
    nb = S // Q_BLOCK
    q = apply_rope(q.transpose(0, 2, 3, 1, 4), cos, sin)
    k = apply_rope(k.transpose(0, 2, 3, 1, 4), cos, sin)
    v = v.transpose(0, 2, 1, 3)
    q_blocks = q.reshape(B, H, 2, nb, Q_BLOCK, d).transpose(3, 0, 1, 2, 4, 5)
    scale = d ** -0.5

    def block(qb):
        s = jnp.einsum('bhmqd,bhmkd->bhmqk', qb, k).astype(jnp.float32) * scale
        p = jax.nn.softmax(s, axis=-1)
        w = p[:, :, 0] - lam * p[:, :, 1]
        return jnp.einsum('bhqk,bhke->bhqe', w.astype(v.dtype), v)

    o = lax.map(block, q_blocks)
    o = rms_norm(o, subln_g) * (1.0 - lambda_init)
    return o.transpose(1, 0, 3, 2, 4).reshape(B, S, H * 2 * d)


def latent_attention(c_q, c_kv, k_r, q_norm_g, w_uq, kv_norm_g, w_ukv, cos, sin):
    B, S, _ = c_q.shape
    H = B_HEADS
    nb = S // Q_BLOCK
    q = (rms_norm(c_q, q_norm_g) @ w_uq).reshape(B, S, H, B_NOPE + B_ROPE).transpose(0, 2, 1, 3)
    q_nope = q[..., :B_NOPE]
    q_rope = apply_rope(q[..., B_NOPE:], cos, sin)
    kv = (rms_norm(c_kv, kv_norm_g) @ w_ukv).reshape(B, S, H, B_NOPE + B_V).transpose(0, 2, 1, 3)
    k_nope = kv[..., :B_NOPE]
    v = kv[..., B_NOPE:]
    k_rope = apply_rope(k_r, cos, sin)
    qn_blocks = q_nope.reshape(B, H, nb, Q_BLOCK, B_NOPE).transpose(2, 0, 1, 3, 4)
    qr_blocks = q_rope.reshape(B, H, nb, Q_BLOCK, B_ROPE).transpose(2, 0, 1, 3, 4)
    scale = (B_NOPE + B_ROPE) ** -0.5

    def block(args):
        qn, qr = args
        s = (jnp.einsum('bhqd,bhkd->bhqk', qn, k_nope)
             + jnp.einsum('bhqr,bkr->bhqk', qr, k_rope)).astype(jnp.float32) * scale
        p = jax.nn.softmax(s, axis=-1)
        return jnp.einsum('bhqk,bhke->bhqe', p.astype(v.dtype), v)

    o = lax.map(block, (qn_blocks, qr_blocks))
    return o.transpose(1, 0, 3, 2, 4).reshape(B, S, H * B_V)


def neighbourhood_attention(q, k, v, rel_bias):
    B, S, H, d = q.shape
    rows = S // GRID_W
    kh = min(NA_ROWS, rows)
    kw = NA_COLS
    to_grid = lambda t: t.reshape(B, rows, GRID_W, H, d).transpose(0, 3, 1, 2, 4)
    qg, kg, vg = to_grid(q), to_grid(k), to_grid(v)
    cols = jnp.arange(GRID_W)
    col_start = jnp.clip(cols - kw // 2, 0, GRID_W - kw)
    col_idx = col_start[:, None] + jnp.arange(kw)[None, :]
    col_off = col_idx - cols[:, None] + (NA_COLS - 1)
    scale = d ** -0.5

    def row(args):
        r, q_row = args
        r0 = jnp.clip(r - kh // 2, 0, rows - kh)
        k_band = lax.dynamic_slice_in_dim(kg, r0, kh, axis=2)
        v_band = lax.dynamic_slice_in_dim(vg, r0, kh, axis=2)
        k_win = k_band[:, :, :, col_idx]
        v_win = v_band[:, :, :, col_idx]
        row_off = r0 + jnp.arange(kh) - r + (NA_ROWS - 1)
        bias = rel_bias[:, row_off[:, None, None], col_off[None]]
        bias = bias.transpose(0, 2, 1, 3).astype(jnp.float32)
        s = jnp.einsum('bhcd,bhicjd->bhcij', q_row, k_win).astype(jnp.float32) * scale + bias[None]
        p = jax.nn.softmax(s.reshape(B, H, GRID_W, kh * kw), axis=-1).reshape(B, H, GRID_W, kh, kw)
        return jnp.einsum('bhcij,bhicje->bhce', p.astype(v_win.dtype), v_win)

    o = lax.map(row, (jnp.arange(rows), qg.transpose(2, 0, 1, 3, 4)))
    return o.transpose(1, 0, 3, 2, 4).reshape(B, S, H * d)


def even_mixer(x, w_in, lam_q1, lam_k1, lam_q2, lam_k2, subln_g, q_norm_g, w_uq,
               kv_norm_g, w_ukv, w_o, lambda_init, rope_a, rope_b):
    B, S, _ = x.shape
    h = x @ w_in
    qa, ka, va, cq, ckv, kr = jnp.split(h, EVEN_SPLITS, axis=-1)
    f32 = jnp.float32
    lam = (jnp.exp(jnp.sum(lam_q1.astype(f32) * lam_k1.astype(f32)))
           - jnp.exp(jnp.sum(lam_q2.astype(f32) * lam_k2.astype(f32))) + lambda_init)
    a_out = diff_attention(qa.reshape(B, S, A_HEADS, 2, A_HEAD_DIM),
                           ka.reshape(B, S, A_HEADS, 2, A_HEAD_DIM),
                           va.reshape(B, S, A_HEADS, 2 * A_HEAD_DIM),
                           lam, subln_g, lambda_init, rope_a[0], rope_a[1])
    b_out = latent_attention(cq, ckv, kr, q_norm_g, w_uq, kv_norm_g, w_ukv, rope_b[0], rope_b[1])
    return jnp.concatenate([a_out, b_out], axis=-1) @ w_o


def odd_mixer(x, w_qkv, rel_bias, w_o):
    B, S, _ = x.shape
    q, k, v = jnp.split(x @ w_qkv, 3, axis=-1)
    shp = (B, S, C_HEADS, C_HEAD_DIM)
    o = neighbourhood_attention(q.reshape(shp), k.reshape(shp), v.reshape(shp), rel_bias)
    return o @ w_o


def hier_moe(x, w_rg, b_rg, w_re, b_re, w1, w3, w2):
    B, S, D = x.shape
    N = B * S
    xt = x.reshape(N, D)
    g_prob = jax.nn.softmax((xt @ w_rg).astype(jnp.float32) + b_rg.astype(jnp.float32), axis=-1)
    grp = jnp.argmax(g_prob, axis=-1).astype(jnp.int32)
    g_w = jnp.take_along_axis(g_prob, grp[:, None], axis=-1)
    e_logits = ((xt @ w_re).astype(jnp.float32) + b_re.astype(jnp.float32)).reshape(N, N_GROUPS, EXPERTS_PER_GROUP)
    idx = jnp.broadcast_to(grp[:, None, None], (N, 1, EXPERTS_PER_GROUP))
    e_logits = jnp.take_along_axis(e_logits, idx, axis=1)[:, 0]
    top_v, top_i = lax.top_k(e_logits, TOP_K)
    gate = jax.nn.softmax(top_v, axis=-1) * g_w
    expert = grp[:, None] * EXPERTS_PER_GROUP + top_i.astype(jnp.int32)
    A = N * TOP_K
    e_flat = expert.reshape(A)
    w_flat = gate.reshape(A)
    tok = jnp.repeat(jnp.arange(N, dtype=jnp.int32), TOP_K)
    order = jnp.argsort(e_flat)
    e_s, tok_s, w_s = e_flat[order], tok[order], w_flat[order]
    counts = jnp.bincount(e_flat, length=N_EXPERTS)
    start = jnp.cumsum(counts) - counts
    padded = (counts + MOE_BLOCK - 1) // MOE_BLOCK * MOE_BLOCK
    pad_end = jnp.cumsum(padded)
    pad_start = pad_end - padded
    dest = pad_start[e_s] + jnp.arange(A) - start[e_s]
    n_blocks = -(-A // MOE_BLOCK) + N_EXPERTS
    P = n_blocks * MOE_BLOCK
    slot_tok = jnp.full((P,), N, jnp.int32).at[dest].set(tok_s)
    slot_w = jnp.zeros((P,), jnp.float32).at[dest].set(w_s)
    block_exp = jnp.minimum(jnp.searchsorted(pad_end, jnp.arange(n_blocks) * MOE_BLOCK, side='right'),
                            N_EXPERTS - 1).astype(jnp.int32)
    x_pad = jnp.concatenate([xt, jnp.zeros((1, D), xt.dtype)], axis=0)

    def block(args):
        tids, e, w = args
        xb = x_pad[tids]
        hdn = jax.nn.silu(xb @ w1[e]) * (xb @ w3[e])
        return (hdn @ w2[e]) * w[:, None].astype(xb.dtype)

    y = lax.map(block, (slot_tok.reshape(n_blocks, MOE_BLOCK), block_exp,
                        slot_w.reshape(n_blocks, MOE_BLOCK)))
    out = jax.ops.segment_sum(y.reshape(P, D), slot_tok, num_segments=N + 1)[:N]
    return out.reshape(B, S, D)


def setup_inputs(seed: int = 0) -> dict:
    key = jax.random.key(seed)
    ks = iter(jax.random.split(key, 32))
    nrm = lambda shape, scale: jax.random.normal(next(ks), shape, jnp.float32) * scale
    gain = lambda shape: 1.0 + nrm(shape, 0.02)
    D = D_MODEL
    return {
        "x": nrm((BATCH, SEQ, D), 1.0),
        "even_w_in": nrm((N_EVEN, D, EVEN_IN), D ** -0.5),
        "even_lam_q1": nrm((N_EVEN, A_HEAD_DIM), 0.1),
        "even_lam_k1": nrm((N_EVEN, A_HEAD_DIM), 0.1),
        "even_lam_q2": nrm((N_EVEN, A_HEAD_DIM), 0.1),
        "even_lam_k2": nrm((N_EVEN, A_HEAD_DIM), 0.1),
        "even_subln_g": gain((N_EVEN, 2 * A_HEAD_DIM)),
        "even_q_norm_g": gain((N_EVEN, B_Q_RANK)),
        "even_w_uq": nrm((N_EVEN, B_Q_RANK, B_HEADS * (B_NOPE + B_ROPE)), B_Q_RANK ** -0.5),
        "even_kv_norm_g": gain((N_EVEN, B_KV_RANK)),
        "even_w_ukv": nrm((N_EVEN, B_KV_RANK, B_HEADS * (B_NOPE + B_V)), B_KV_RANK ** -0.5),
        "even_w_o": nrm((N_EVEN, EVEN_OUT, D), EVEN_OUT ** -0.5 * DN_BETA),
        "odd_w_qkv": nrm((N_ODD, D, 3 * C_WIDTH), D ** -0.5),
        "odd_rel_bias": nrm((N_ODD, C_HEADS, 2 * NA_ROWS - 1, 2 * NA_COLS - 1), 0.02),
        "odd_w_o": nrm((N_ODD, C_WIDTH, D), C_WIDTH ** -0.5 * DN_BETA),
        "ln1_g": gain((DEPTH, D)),
        "ln1_b": nrm((DEPTH, D), 0.02),
        "ln2_g": gain((DEPTH, D)),
        "ln2_b": nrm((DEPTH, D), 0.02),
        "w_router_group": nrm((DEPTH, D, N_GROUPS), D ** -0.5),
        "b_router_group": nrm((DEPTH, N_GROUPS), 0.01),
        "w_router_expert": nrm((DEPTH, D, N_EXPERTS), D ** -0.5),
        "b_router_expert": nrm((DEPTH, N_EXPERTS), 0.01),
        "w_gate": nrm((DEPTH, N_EXPERTS, D, D_EXPERT), D ** -0.5),
        "w_up": nrm((DEPTH, N_EXPERTS, D, D_EXPERT), D ** -0.5),
        "w_down": nrm((DEPTH, N_EXPERTS, D_EXPERT, D), D_EXPERT ** -0.5 * DN_BETA),
    }


def reference(x, even_w_in, even_lam_q1, even_lam_k1, even_lam_q2, even_lam_k2, even_subln_g,
              even_q_norm_g, even_w_uq, even_kv_norm_g, even_w_ukv, even_w_o,
              odd_w_qkv, odd_rel_bias, odd_w_o, ln1_g, ln1_b, ln2_g, ln2_b,
              w_router_group, b_router_group, w_router_expert, b_router_expert,
              w_gate, w_up, w_down):
    S = x.shape[1]
    rope_a = rope_tables(S, A_ROT)
    rope_b = rope_tables(S, B_ROPE)
    for i in range(DEPTH):
        j = i // 2
        if i % 2 == 0:
            lambda_init = 0.8 - 0.6 * math.exp(-0.3 * i)
            m = even_mixer(x, even_w_in[j], even_lam_q1[j], even_lam_k1[j], even_lam_q2[j],
                           even_lam_k2[j], even_subln_g[j], even_q_norm_g[j], even_w_uq[j],
                           even_kv_norm_g[j], even_w_ukv[j], even_w_o[j], lambda_init, rope_a, rope_b)
        else:
            m = odd_mixer(x, odd_w_qkv[j], odd_rel_bias[j], odd_w_o[j])
        x = layer_norm(DN_ALPHA * x + m, ln1_g[i], ln1_b[i])
        f = hier_moe(x, w_router_group[i], b_router_group[i], w_router_expert[i], b_router_expert[i],
                     w_gate[i], w_up[i], w_down[i])
        x = layer_norm(DN_ALPHA * x + f, ln2_g[i], ln2_b[i])
    return x
```

```python
import functools
import math

import numpy as np
import jax
import jax.numpy as jnp
from jax import lax
from jax.experimental import pallas as pl
from jax.experimental.pallas import tpu as pltpu

F32 = jnp.float32
BF16 = jnp.bfloat16

D_MODEL = 1024
DEPTH = 4
GRID_W = 64
ROPE_THETA = 500000.0
A_HEADS = 4
A_HEAD_DIM = 64
A_ROT = A_HEAD_DIM // 4
A_W = A_HEADS * 2 * A_HEAD_DIM
B_HEADS = 8
B_NOPE = 64
B_ROPE = 32
B_V = 64
B_Q_RANK = 256
B_KV_RANK = 128
C_HEADS = 16
C_HEAD_DIM = 64
C_WIDTH = C_HEADS * C_HEAD_DIM
NA_ROWS = 8
NA_COLS = 16
N_GROUPS = 4
EXPERTS_PER_GROUP = 8
N_EXPERTS = N_GROUPS * EXPERTS_PER_GROUP
TOP_K = 2
D_EXPERT = 512
DN_ALPHA = (2 * DEPTH) ** 0.25
LN_EPS = 1e-5
RMS_EPS = 1e-6

LANES = 128
VMEM_LIMIT_BYTES = 52 * 1024 * 1024

EVEN_IN_PAD = 2048
KR_LANE0 = 64
TOKEN_TILE = 512
ATTN_Q_TILE = 512
NA_Q_ROWS = 4
NA_Q_TILE = NA_Q_ROWS * GRID_W
NA_BAND_CHUNKS = 3
NA_BATCH_TILE = 4
MOE_BLOCK = 256
NEG_BIG = -1e30
ROUTE_LANES = LANES


def _cparams(sem):
    return pltpu.CompilerParams(dimension_semantics=sem, vmem_limit_bytes=VMEM_LIMIT_BYTES)


def _rope_block_tables(seq, rot_dim, lane_starts, scale):
    half = rot_dim // 2
    inv_freq = ROPE_THETA ** (-jnp.arange(0, rot_dim, 2, dtype=F32) / rot_dim)
    ang = jnp.arange(seq, dtype=F32)[:, None] * inv_freq[None, :]
    cos, sin = jnp.cos(ang), jnp.sin(ang)
    c = jnp.ones((seq, LANES), F32)
    a = jnp.zeros((seq, LANES), F32)
    b = jnp.zeros((seq, LANES), F32)
    for s0 in lane_starts:
        c = c.at[:, s0:s0 + half].set(cos).at[:, s0 + half:s0 + 2 * half].set(cos)
        a = a.at[:, s0:s0 + half].set(-sin)
        b = b.at[:, s0 + half:s0 + 2 * half].set(sin)
    return jnp.stack([c, a, b]) * scale


def _rope128(x, c, a, b, half):
    return x * c + pltpu.roll(x, LANES - half, 1) * a + pltpu.roll(x, half, 1) * b


def _rms(x, g):
    return x * lax.rsqrt(jnp.mean(x * x, axis=-1, keepdims=True) + RMS_EPS) * g


def _even_proj_kernel(x_ref, win_ref, wuq_ref, wk_ref, wv_ref, qg_ref, kvg_ref, t_ref,
                      qkva_ref, qb_ref, kb_ref, vb_ref):
    xb = x_ref[...].astype(BF16)
    h = jnp.dot(xb, win_ref[...], preferred_element_type=F32)
    ha = A_ROT // 2
    hb = B_ROPE // 2
    for hd in range(A_HEADS):
        lo = hd * LANES
        qkva_ref[:, lo:lo + LANES] = _rope128(h[:, lo:lo + LANES], t_ref[0], t_ref[1], t_ref[2], ha).astype(BF16)
        lo = A_W + hd * LANES
        qkva_ref[:, lo:lo + LANES] = _rope128(h[:, lo:lo + LANES], t_ref[3], t_ref[4], t_ref[5], ha).astype(BF16)
    qkva_ref[:, 2 * A_W:3 * A_W] = h[:, 2 * A_W:3 * A_W].astype(BF16)
    c0 = 3 * A_W
    cq = _rms(h[:, c0:c0 + B_Q_RANK], qg_ref[...])
    qb = jnp.dot(cq.astype(BF16), wuq_ref[...], preferred_element_type=F32)
    ckv = _rms(h[:, c0 + B_Q_RANK:c0 + B_Q_RANK + B_KV_RANK], kvg_ref[...]).astype(BF16)
    kn = jnp.dot(ckv, wk_ref[...], preferred_element_type=F32)
    vb_ref[...] = jnp.dot(ckv, wv_ref[...], preferred_element_type=F32).astype(BF16)
    kr = _rope128(h[:, EVEN_IN_PAD - LANES:], t_ref[9], t_ref[10], t_ref[11], hb)
    for hd in range(B_HEADS):
        lo = hd * LANES
        qb_ref[:, lo:lo + LANES] = _rope128(qb[:, lo:lo + LANES], t_ref[6], t_ref[7], t_ref[8], hb).astype(BF16)
        kb_ref[:, lo:lo + LANES] = (kn[:, lo:lo + LANES] + kr).astype(BF16)


def _even_proj(x2d, win, wuq, wk, wv, qg, kvg, tabs, seq):
    n = x2d.shape[0]
    tm = TOKEN_TILE
    assert n % tm == 0 and seq % tm == 0
    spb = seq // tm
    full = lambda shape: pl.BlockSpec(shape, lambda i: (0,) * len(shape))
    row = lambda w: pl.BlockSpec((tm, w), lambda i: (i, 0))
    return pl.pallas_call(
        _even_proj_kernel,
        grid=(n // tm,),
        in_specs=[row(D_MODEL), full(win.shape), full(wuq.shape), full(wk.shape), full(wv.shape),
                  full(qg.shape), full(kvg.shape),
                  pl.BlockSpec((12, tm, LANES), lambda i: (0, i % spb, 0))],
        out_specs=[row(3 * A_W), row(B_HEADS * LANES), row(B_HEADS * LANES), row(B_HEADS * B_V)],
        out_shape=[jax.ShapeDtypeStruct((n, 3 * A_W), BF16),
                   jax.ShapeDtypeStruct((n, B_HEADS * LANES), BF16),
                   jax.ShapeDtypeStruct((n, B_HEADS * LANES), BF16),
                   jax.ShapeDtypeStruct((n, B_HEADS * B_V), BF16)],
        compiler_params=_cparams(("parallel",)),
        name="even_proj",
    )(x2d, win, wuq, wk, wv, qg, kvg, tabs)


_NT = (((1,), (1,)), ((), ()))


def _softmax_pv(q, k, v, bias=None):
    s = lax.dot_general(q, k, _NT, preferred_element_type=F32)
    if bias is not None:
        s = s + bias
    m = jnp.max(s, axis=-1, keepdims=True)
    e = jnp.exp(s - m)
    l = jnp.sum(e, axis=-1, keepdims=True)
    o = jnp.dot(e.astype(BF16), v, preferred_element_type=F32)
    return o / l


def _diff_attn_kernel(lam_ref, q_ref, k_ref, v_ref, g_ref, o_ref, *, out_scale):
    q = q_ref[0]
    k = k_ref[0]
    v = v_ref[0]
    lane = lax.broadcasted_iota(jnp.int32, q.shape, 1)
    zero = jnp.zeros_like(q)
    o1 = _softmax_pv(jnp.where(lane < A_HEAD_DIM, q, zero), k, v)
    o2 = _softmax_pv(jnp.where(lane >= A_HEAD_DIM, q, zero), k, v)
    o = o1 - lam_ref[0] * o2
    o_ref[0] = (_rms(o, g_ref[...]) * out_scale).astype(BF16)


def _diff_attn(qkva, lam, subln_g, out_scale):
    b, s, _ = qkva.shape
    tq = ATTN_Q_TILE
    return pl.pallas_call(
        functools.partial(_diff_attn_kernel, out_scale=out_scale),
        grid=(b, A_HEADS, s // tq),
        in_specs=[pl.BlockSpec(memory_space=pltpu.SMEM),
                  pl.BlockSpec((1, tq, LANES), lambda bi, h, qi: (bi, qi, h)),
                  pl.BlockSpec((1, s, LANES), lambda bi, h, qi: (bi, 0, A_HEADS + h)),
                  pl.BlockSpec((1, s, LANES), lambda bi, h, qi: (bi, 0, 2 * A_HEADS + h)),
                  pl.BlockSpec((1, LANES), lambda bi, h, qi: (0, 0))],
        out_specs=pl.BlockSpec((1, tq, LANES), lambda bi, h, qi: (bi, qi, h)),
        out_shape=jax.ShapeDtypeStruct((b, s, A_W), BF16),
        compiler_params=_cparams(("parallel", "parallel", "parallel")),
        name="diff_attn",
    )(lam, qkva, qkva, qkva, subln_g)


def _mla_attn_kernel(q_ref, k_ref, v_ref, o_ref):
    v = v_ref[0]
    outs = []
    for j in range(2):
        q = q_ref[0, :, j * LANES:(j + 1) * LANES]
        k = k_ref[0, :, j * LANES:(j + 1) * LANES]
        outs.append(_softmax_pv(q, k, v))
    lane = lax.broadcasted_iota(jnp.int32, outs[0].shape, 1)
    o_ref[0] = jnp.where(lane < B_V, outs[0], outs[1]).astype(BF16)


def _mla_attn(qb, kb, vb):
    b, s, _ = qb.shape
    tq = ATTN_Q_TILE
    return pl.pallas_call(
        _mla_attn_kernel,
        grid=(b, B_HEADS // 2, s // tq),
        in_specs=[pl.BlockSpec((1, tq, 2 * LANES), lambda bi, p, qi: (bi, qi, p)),
                  pl.BlockSpec((1, s, 2 * LANES), lambda bi, p, qi: (bi, 0, p)),
                  pl.BlockSpec((1, s, LANES), lambda bi, p, qi: (bi, 0, p))],
        out_specs=pl.BlockSpec((1, tq, LANES), lambda bi, p, qi: (bi, qi, p)),
        out_shape=jax.ShapeDtypeStruct((b, s, B_HEADS * B_V), BF16),
        compiler_params=_cparams(("parallel", "parallel", "parallel")),
        name="mla_attn",
    )(qb, kb, vb)


def _proj_kernel(x_ref, w_ref, o_ref):
    o_ref[...] = jnp.dot(x_ref[...].astype(BF16), w_ref[...], preferred_element_type=F32).astype(o_ref.dtype)


def _proj(x2d, w):
    n, kdim = x2d.shape
    m = w.shape[1]
    tm = TOKEN_TILE
    return pl.pallas_call(
        _proj_kernel,
        grid=(n // tm,),
        in_specs=[pl.BlockSpec((tm, kdim), lambda i: (i, 0)), pl.BlockSpec((kdim, m), lambda i: (0, 0))],
        out_specs=pl.BlockSpec((tm, m), lambda i: (i, 0)),
        out_shape=jax.ShapeDtypeStruct((n, m), BF16),
        compiler_params=_cparams(("parallel",)),
        name="qkv_proj",
    )(x2d, w)


def _na_variant_blocks(rows):
    return (0, 1, rows // NA_Q_ROWS - 1)


def _na_band_start(i, rows):
    return jnp.clip(i - 1, 0, rows // NA_Q_ROWS - NA_BAND_CHUNKS)


def _natten_tables(rel_bias, rows):
    band = NA_BAND_CHUNKS * NA_Q_TILE
    a = np.arange(NA_Q_TILE)
    kidx = np.arange(band)
    tabs = []
    for i_rep in _na_variant_blocks(rows):
        bs = int(np.clip(i_rep - 1, 0, rows // NA_Q_ROWS - NA_BAND_CHUNKS)) * NA_Q_ROWS
        qr = NA_Q_ROWS * i_rep + a // GRID_W
        qc = a % GRID_W
        kr = bs + kidx // GRID_W
        kc = kidx % GRID_W
        r0 = np.clip(qr - NA_ROWS // 2, 0, rows - NA_ROWS)
        c0 = np.clip(qc - NA_COLS // 2, 0, GRID_W - NA_COLS)
        inwin = ((kr[None] >= r0[:, None]) & (kr[None] < r0[:, None] + NA_ROWS)
                 & (kc[None] >= c0[:, None]) & (kc[None] < c0[:, None] + NA_COLS))
        ro = np.clip(kr[None] - qr[:, None] + NA_ROWS - 1, 0, 2 * NA_ROWS - 2)
        co = np.clip(kc[None] - qc[:, None] + NA_COLS - 1, 0, 2 * NA_COLS - 2)
        bias = rel_bias[:, ro, co].astype(F32)
        tabs.append(jnp.where(inwin[None], bias, NEG_BIG))
    return jnp.stack(tabs, axis=1)


def _natten_kernel(q_ref, k0_ref, k1_ref, k2_ref, v0_ref, v1_ref, v2_ref, t_ref, o_ref, *, bt):
    lane = lax.broadcasted_iota(jnp.int32, (NA_Q_TILE, LANES), 1)
    lo = lane < C_HEAD_DIM
    for b in range(bt):
        q = q_ref[b] * (C_HEAD_DIM ** -0.5)
        zero = jnp.zeros_like(q)
        k = jnp.concatenate([k0_ref[b], k1_ref[b], k2_ref[b]], axis=0)
        v = jnp.concatenate([v0_ref[b], v1_ref[b], v2_ref[b]], axis=0)
        o0 = _softmax_pv(jnp.where(lo, q, zero), k, v, t_ref[0])
        o1 = _softmax_pv(jnp.where(lo, zero, q), k, v, t_ref[1])
        o_ref[b] = jnp.where(lo, o0, o1).astype(BF16)


def _natten(qkv, tabs):
    b, s, _ = qkv.shape
    rows = s // GRID_W
    nblk = rows // NA_Q_ROWS
    bt = NA_BATCH_TILE if b % NA_BATCH_TILE == 0 else 1
    npair = C_HEADS // 2
    blk = (bt, NA_Q_TILE, LANES)

    def band_spec(col0, j):
        return pl.BlockSpec(blk, lambda p, i, bi: (bi, _na_band_start(i, rows) + j, col0 + p))

    def variant(i):
        return jnp.where(i == 0, 0, jnp.where(i == nblk - 1, 2, 1))

    return pl.pallas_call(
        functools.partial(_natten_kernel, bt=bt),
        grid=(npair, nblk, b // bt),
        in_specs=[pl.BlockSpec(blk, lambda p, i, bi: (bi, i, p))]
                 + [band_spec(npair, j) for j in range(NA_BAND_CHUNKS)]
                 + [band_spec(2 * npair, j) for j in range(NA_BAND_CHUNKS)]
                 + [pl.BlockSpec((2, None, NA_Q_TILE, NA_BAND_CHUNKS * NA_Q_TILE),
                                 lambda p, i, bi: (p, variant(i), 0, 0))],
        out_specs=pl.BlockSpec(blk, lambda p, i, bi: (bi, i, p)),
        out_shape=jax.ShapeDtypeStruct((b, s, C_WIDTH), BF16),
        compiler_params=_cparams(("parallel", "parallel", "parallel")),
        name="natten",
    )(qkv, qkv, qkv, qkv, qkv, qkv, qkv, tabs)


def _layer_norm(z, g, b):
    mu = jnp.mean(z, axis=-1, keepdims=True)
    zc = z - mu
    var = jnp.mean(zc * zc, axis=-1, keepdims=True)
    return zc * lax.rsqrt(var + LN_EPS) * g + b


def _route(logits):
    lane_i = lax.broadcasted_iota(jnp.int32, logits.shape, 1)
    lane = lane_i.astype(F32)
    big = float(LANES)
    gl = jnp.where(lane_i < N_GROUPS, logits, NEG_BIG)
    gmax = jnp.max(gl, axis=-1, keepdims=True)
    grp = jnp.min(jnp.where(gl == gmax, lane, big), axis=-1, keepdims=True)
    g_w = 1.0 / jnp.sum(jnp.exp(gl - gmax), axis=-1, keepdims=True)
    e_lo = grp * EXPERTS_PER_GROUP + N_GROUPS
    in_grp = jnp.logical_and(lane >= e_lo, lane < e_lo + EXPERTS_PER_GROUP)
    el = jnp.where(in_grp, logits, NEG_BIG)
    v1 = jnp.max(el, axis=-1, keepdims=True)
    i1 = jnp.min(jnp.where(el == v1, lane, big), axis=-1, keepdims=True)
    el2 = jnp.where(lane == i1, NEG_BIG, el)
    v2 = jnp.max(el2, axis=-1, keepdims=True)
    i2 = jnp.min(jnp.where(el2 == v2, lane, big), axis=-1, keepdims=True)
    ex = jnp.exp(v2 - v1)
    den = 1.0 + ex
    gate1 = (1.0 / den) * g_w
    gate2 = (ex / den) * g_w
    rec = jnp.where(lane_i == 0, i1 - N_GROUPS,
                    jnp.where(lane_i == 1, i2 - N_GROUPS,
                              jnp.where(lane_i == 2, gate1, jnp.where(lane_i == 3, gate2, 0.0))))
    return rec


def _post_attn_kernel(*refs, n_act):
    acts = refs[:n_act]
    ws = refs[n_act:2 * n_act]
    x_ref, g_ref, b_ref, wrh_ref, wrl_ref, br_ref, x1_ref, route_ref = refs[2 * n_act:]
    m = jnp.dot(acts[0][...], ws[0][...], preferred_element_type=F32)
    for a_ref, w_ref in zip(acts[1:], ws[1:]):
        m = m + jnp.dot(a_ref[...], w_ref[...], preferred_element_type=F32)
    y = _layer_norm(DN_ALPHA * x_ref[...] + m, g_ref[...], b_ref[...])
    x1_ref[...] = y
    y_hi = y.astype(BF16)
    y_lo = (y - y_hi.astype(F32)).astype(BF16)
    wrh = wrh_ref[...]
    logits = (jnp.dot(y_hi, wrh, preferred_element_type=F32)
              + jnp.dot(y_lo, wrh, preferred_element_type=F32)
              + jnp.dot(y_hi, wrl_ref[...], preferred_element_type=F32)) + br_ref[...]
    route_ref[...] = _route(logits)


def _post_attn(acts, ws, x2d, g, b, wrh, wrl, br):
    n = x2d.shape[0]
    tm = TOKEN_TILE
    full = lambda arr: pl.BlockSpec(arr.shape, lambda i: (0,) * arr.ndim)
    row = lambda w: pl.BlockSpec((tm, w), lambda i: (i, 0))
    return pl.pallas_call(
        functools.partial(_post_attn_kernel, n_act=len(acts)),
        grid=(n // tm,),
        in_specs=[row(a.shape[1]) for a in acts] + [full(w) for w in ws]
                 + [row(D_MODEL), full(g), full(b), full(wrh), full(wrl), full(br)],
        out_specs=[row(D_MODEL), row(ROUTE_LANES)],
        out_shape=[jax.ShapeDtypeStruct((n, D_MODEL), F32), jax.ShapeDtypeStruct((n, ROUTE_LANES), F32)],
        compiler_params=_cparams(("parallel",)),
        name="post_attn",
    )(*acts, *ws, x2d, g, b, wrh, wrl, br)


def _moe_plan(route, n_tok, bm):
    n_assign = n_tok * TOP_K
    nb = n_assign // bm + N_EXPERTS
    e_flat = route[:, :TOP_K].astype(jnp.int32).reshape(n_assign)
    order = jnp.argsort(e_flat, stable=True).astype(jnp.int32)
    counts = jnp.sum((e_flat[:, None] == jnp.arange(N_EXPERTS, dtype=jnp.int32)[None, :]).astype(jnp.int32), axis=0)
    start = jnp.cumsum(counts) - counts
    padded = (counts + bm - 1) // bm * bm
    pad_end = jnp.cumsum(padded)
    pad_start = pad_end - padded
    n_used = (pad_end[-1] // bm).astype(jnp.int32)
    blk = jnp.arange(nb, dtype=jnp.int32)
    bexp = jnp.minimum(jnp.searchsorted(pad_end, blk * bm, side='right'), N_EXPERTS - 1).astype(jnp.int32)
    bexp = jnp.where(blk < n_used, bexp, bexp[jnp.maximum(n_used - 1, 0)])
    slot = jnp.arange(nb * bm, dtype=jnp.int32)
    se = jnp.repeat(bexp, bm)
    j = slot - pad_start[se]
    valid = (j < counts[se]) & (slot < n_used * bm)
    a = order[jnp.clip(start[se] + j, 0, n_assign - 1)]
    slot_tok = jnp.where(valid, a // TOP_K, 0).astype(jnp.int32)
    pad_rank = jnp.cumsum((~valid).astype(jnp.int32)) - 1
    slot_dst = jnp.where(valid, a, n_assign + pad_rank).astype(jnp.int32)
    return bexp, n_used.reshape(1), slot_tok.reshape(nb, 1, bm), slot_dst.reshape(nb, 1, bm)


def _expert_kernel(bexp_ref, nused_ref, tokc_ref, tokn_ref, dst_ref, x_hbm, w1_ref, w3_ref, w2_ref,
                   out_hbm, xbuf, ybuf, zbuf, w13, w2b, gsem, ssem, zsem, *, bm):
    i = pl.program_id(0)
    nb = pl.num_programs(0)
    n_used = nused_ref[0]
    slot = i % 2

    def gather_rows(tok_ref, s):
        def body(r, c):
            pltpu.make_async_copy(x_hbm.at[pl.ds(tok_ref[0, 0, r], 1)], xbuf.at[s, pl.ds(r, 1)], gsem.at[s]).start()
            return c
        lax.fori_loop(0, bm, body, 0)

    def wait_gather(s):
        pltpu.make_async_copy(xbuf.at[s], xbuf.at[s], gsem.at[s]).wait()

    def wait_scatter(s):
        pltpu.make_async_copy(ybuf.at[s], ybuf.at[s], ssem.at[s]).wait()

    @pl.when(jnp.logical_and(i == 0, n_used > 0))
    def _():
        gather_rows(tokc_ref, 0)

    @pl.when(i + 1 < n_used)
    def _():
        gather_rows(tokn_ref, 1 - slot)

    @pl.when(i < n_used)
    def _():
        changed = jnp.logical_or(i == 0, bexp_ref[i] != bexp_ref[jnp.maximum(i - 1, 0)])

        @pl.when(changed)
        def _():
            w13[:, :D_EXPERT] = w1_ref[...].astype(BF16)
            w13[:, D_EXPERT:] = w3_ref[...].astype(BF16)
            w2b[...] = w2_ref[...].astype(BF16)

        wait_gather(slot)
        xb = xbuf[slot].astype(BF16)
        h = jnp.dot(xb, w13[...], preferred_element_type=F32)
        gate = h[:, :D_EXPERT]
        act = gate * (1.0 / (1.0 + jnp.exp(-gate))) * h[:, D_EXPERT:]
        y = jnp.dot(act.astype(BF16), w2b[...], preferred_element_type=F32)

        @pl.when(i >= 2)
        def _():
            wait_scatter(slot)

        ybuf[slot] = y

        def body(r, c):
            pltpu.make_async_copy(ybuf.at[slot, pl.ds(r, 1)], out_hbm.at[pl.ds(dst_ref[0, 0, r], 1)], ssem.at[slot]).start()
            return c
        lax.fori_loop(0, bm, body, 0)

    @pl.when(i == 0)
    def _():
        zbuf[...] = jnp.zeros_like(zbuf)

    @pl.when(i >= n_used)
    def _():
        row0 = pl.multiple_of(dst_ref[0, 0, 0], bm)
        cp = pltpu.make_async_copy(zbuf, out_hbm.at[pl.ds(row0, bm)], zsem)
        cp.start()
        cp.wait()

    @pl.when(i == nb - 1)
    def _():
        @pl.when(n_used >= 1)
        def _():
            wait_scatter((n_used - 1) % 2)

        @pl.when(n_used >= 2)
        def _():
            wait_scatter(n_used % 2)


def _experts(x1, plan, w_gate, w_up, w_down, layer):
    bexp, n_used, slot_tok, slot_dst = plan
    nb, _, bm = slot_tok.shape
    smem_blk = lambda fn: pl.BlockSpec((1, 1, bm), fn, memory_space=pltpu.SMEM)
    wspec = lambda k, m: pl.BlockSpec((None, None, k, m), lambda i, be, nu: (layer, be[i], 0, 0))
    grid_spec = pltpu.PrefetchScalarGridSpec(
        num_scalar_prefetch=2,
        grid=(nb,),
        in_specs=[smem_blk(lambda i, be, nu: (i, 0, 0)),
                  smem_blk(lambda i, be, nu: (jnp.minimum(i + 1, nb - 1), 0, 0)),
                  smem_blk(lambda i, be, nu: (i, 0, 0)),
                  pl.BlockSpec(memory_space=pl.ANY),
                  wspec(D_MODEL, D_EXPERT), wspec(D_MODEL, D_EXPERT), wspec(D_EXPERT, D_MODEL)],
        out_specs=pl.BlockSpec(memory_space=pl.ANY),
        scratch_shapes=[pltpu.VMEM((2, bm, D_MODEL), F32), pltpu.VMEM((2, bm, D_MODEL), F32),
                        pltpu.VMEM((bm, D_MODEL), F32),
                        pltpu.VMEM((D_MODEL, 2 * D_EXPERT), BF16), pltpu.VMEM((D_EXPERT, D_MODEL), BF16),
                        pltpu.SemaphoreType.DMA((2,)), pltpu.SemaphoreType.DMA((2,)), pltpu.SemaphoreType.DMA(())],
    )
    return pl.pallas_call(
        functools.partial(_expert_kernel, bm=bm),
        grid_spec=grid_spec,
        out_shape=jax.ShapeDtypeStruct((nb * bm, D_MODEL), F32),
        compiler_params=_cparams(("arbitrary",)),
        name="experts",
    )(bexp, n_used, slot_tok, slot_tok, slot_dst, x1, w_gate, w_up, w_down)


def _post_moe_kernel(x1_ref, y0_ref, y1_ref, route_ref, g_ref, b_ref, o_ref):
    r = route_ref[...]
    f = y0_ref[...] * r[:, 2:3] + y1_ref[...] * r[:, 3:4]
    o_ref[...] = _layer_norm(DN_ALPHA * x1_ref[...] + f, g_ref[...], b_ref[...])


def _post_moe(x1, y2, route, g, b):
    n = x1.shape[0]
    tm = TOKEN_TILE
    full = lambda arr: pl.BlockSpec(arr.shape, lambda i: (0,) * arr.ndim)
    return pl.pallas_call(
        _post_moe_kernel,
        grid=(n // tm,),
        in_specs=[pl.BlockSpec((tm, D_MODEL), lambda i: (i, 0)),
                  pl.BlockSpec((tm, D_MODEL), lambda i: (i, 0)),
                  pl.BlockSpec((tm, D_MODEL), lambda i: (i, 1)),
                  pl.BlockSpec((tm, ROUTE_LANES), lambda i: (i, 0)),
                  full(g), full(b)],
        out_specs=pl.BlockSpec((tm, D_MODEL), lambda i: (i, 0)),
        out_shape=jax.ShapeDtypeStruct((n, D_MODEL), F32),
        compiler_params=_cparams(("parallel",)),
        name="post_moe",
    )(x1, y2, y2, route, g, b)


def _prep_even(w_in, w_uq, w_ukv, w_o):
    c0 = 3 * A_W + B_Q_RANK + B_KV_RANK
    win = jnp.zeros((D_MODEL, EVEN_IN_PAD), F32)
    win = win.at[:, :c0].set(w_in[:, :c0])
    kr0 = EVEN_IN_PAD - LANES + KR_LANE0
    win = win.at[:, kr0:kr0 + B_ROPE].set(w_in[:, c0:c0 + B_ROPE])
    wuq = w_uq.reshape(B_Q_RANK, B_HEADS, B_NOPE + B_ROPE)
    wuq = jnp.pad(wuq, ((0, 0), (0, 0), (0, LANES - B_NOPE - B_ROPE))).reshape(B_Q_RANK, B_HEADS * LANES)
    wukv = w_ukv.reshape(B_KV_RANK, B_HEADS, B_NOPE + B_V)
    wk = jnp.pad(wukv[:, :, :B_NOPE], ((0, 0), (0, 0), (0, LANES - B_NOPE))).reshape(B_KV_RANK, B_HEADS * LANES)
    wv = wukv[:, :, B_NOPE:].reshape(B_KV_RANK, B_HEADS * B_V)
    return (win.astype(BF16), wuq.astype(BF16), wk.astype(BF16), wv.astype(BF16),
            w_o[:A_W].astype(BF16), w_o[A_W:].astype(BF16))


def _prep_router(w_rg, b_rg, w_re, b_re):
    wr = jnp.zeros((D_MODEL, ROUTE_LANES), F32)
    wr = wr.at[:, :N_GROUPS].set(w_rg).at[:, N_GROUPS:N_GROUPS + N_EXPERTS].set(w_re)
    br = jnp.zeros((1, ROUTE_LANES), F32)
    br = br.at[0, :N_GROUPS].set(b_rg).at[0, N_GROUPS:N_GROUPS + N_EXPERTS].set(b_re)
    wr_hi = wr.astype(BF16)
    wr_lo = (wr - wr_hi.astype(F32)).astype(BF16)
    return wr_hi, wr_lo, br


def kernel(x, even_w_in, even_lam_q1, even_lam_k1, even_lam_q2, even_lam_k2, even_subln_g, even_q_norm_g, even_w_uq, even_kv_norm_g, even_w_ukv, even_w_o, odd_w_qkv, odd_rel_bias, odd_w_o, ln1_g, ln1_b, ln2_g, ln2_b, w_router_group, b_router_group, w_router_expert, b_router_expert, w_gate, w_up, w_down):
    bsz, seq, d = x.shape
    n = bsz * seq
    rows = seq // GRID_W
    x2d = x.reshape(n, d)

    a_scale = A_HEAD_DIM ** -0.5
    b_scale = (B_NOPE + B_ROPE) ** -0.5
    a_lanes = (0, A_HEAD_DIM)
    rope_tabs = jnp.concatenate([
        _rope_block_tables(seq, A_ROT, a_lanes, a_scale),
        _rope_block_tables(seq, A_ROT, a_lanes, 1.0),
        _rope_block_tables(seq, B_ROPE, (B_NOPE,), b_scale),
        _rope_block_tables(seq, B_ROPE, (KR_LANE0,), 1.0)], axis=0)

    for i in range(DEPTH):
        j = i // 2
        if i % 2 == 0:
            lambda_init = 0.8 - 0.6 * math.exp(-0.3 * i)
            win, wuq, wk, wv, woa, wob = _prep_even(even_w_in[j], even_w_uq[j], even_w_ukv[j], even_w_o[j])
            lam = (jnp.exp(jnp.sum(even_lam_q1[j] * even_lam_k1[j]))
                   - jnp.exp(jnp.sum(even_lam_q2[j] * even_lam_k2[j])) + lambda_init).reshape(1).astype(F32)
            qkva, qb, kb, vb = _even_proj(x2d, win, wuq, wk, wv, even_q_norm_g[j].reshape(1, -1),
                                          even_kv_norm_g[j].reshape(1, -1), rope_tabs, seq)
            a_out = _diff_attn(qkva.reshape(bsz, seq, -1), lam, even_subln_g[j].reshape(1, -1), 1.0 - lambda_init)
            b_out = _mla_attn(qb.reshape(bsz, seq, -1), kb.reshape(bsz, seq, -1), vb.reshape(bsz, seq, -1))
            acts = [a_out.reshape(n, -1), b_out.reshape(n, -1)]
            ws = [woa, wob]
        else:
            qkv = _proj(x2d, odd_w_qkv[j].astype(BF16))
            o = _natten(qkv.reshape(bsz, seq, -1), _natten_tables(odd_rel_bias[j], rows))
            acts = [o.reshape(n, -1)]
            ws = [odd_w_o[j].astype(BF16)]
        wr_hi, wr_lo, br = _prep_router(w_router_group[i], b_router_group[i], w_router_expert[i], b_router_expert[i])
        x1, route = _post_attn(acts, ws, x2d, ln1_g[i].reshape(1, -1), ln1_b[i].reshape(1, -1), wr_hi, wr_lo, br)
        plan = _moe_plan(route, n, MOE_BLOCK)
        y = _experts(x1, plan, w_gate, w_up, w_down, i)
        y2 = y.reshape(y.shape[0] // TOP_K, TOP_K * d)
        x2d = _post_moe(x1, y2, route, ln2_g[i].reshape(1, -1), ln2_b[i].reshape(1, -1))
    return x2d.reshape(bsz, seq, d)
```

```python
import functools
import math

import numpy as np
import jax
import jax.numpy as jnp
from jax import lax
from jax.experimental import pallas as pl
from jax.experimental.pallas import tpu as pltpu

F32 = jnp.float32
BF16 = jnp.bfloat16

D_MODEL = 1024
DEPTH = 4
GRID_W = 64
ROPE_THETA = 500000.0
A_HEADS = 4
A_HEAD_DIM = 64
A_ROT = A_HEAD_DIM // 4
A_W = A_HEADS * 2 * A_HEAD_DIM
B_HEADS = 8
B_NOPE = 64
B_ROPE = 32
B_V = 64
B_Q_RANK = 256
B_KV_RANK = 128
C_HEADS = 16
C_HEAD_DIM = 64
C_WIDTH = C_HEADS * C_HEAD_DIM
NA_ROWS = 8
NA_COLS = 16
N_GROUPS = 4
EXPERTS_PER_GROUP = 8
N_EXPERTS = N_GROUPS * EXPERTS_PER_GROUP
TOP_K = 2
D_EXPERT = 512
DN_ALPHA = (2 * DEPTH) ** 0.25
LN_EPS = 1e-5
RMS_EPS = 1e-6

LANES = 128
VMEM_LIMIT_BYTES = 52 * 1024 * 1024

EVEN_IN_PAD = 2048
KR_LANE0 = 64
TOKEN_TILE = 512
ATTN_Q_TILE = 512
NA_Q_ROWS = 4
NA_Q_TILE = NA_Q_ROWS * GRID_W
NA_BAND_CHUNKS = 3
NA_BATCH_TILE = 4
MOE_BLOCK = 256
NEG_BIG = -1e30
ROUTE_LANES = LANES


def _cparams(sem):
    return pltpu.CompilerParams(dimension_semantics=sem, vmem_limit_bytes=VMEM_LIMIT_BYTES)


def _rope_block_tables(seq, rot_dim, lane_starts, scale):
    half = rot_dim // 2
    inv_freq = ROPE_THETA ** (-jnp.arange(0, rot_dim, 2, dtype=F32) / rot_dim)
    ang = jnp.arange(seq, dtype=F32)[:, None] * inv_freq[None, :]
    cos, sin = jnp.cos(ang), jnp.sin(ang)
    c = jnp.ones((seq, LANES), F32)
    a = jnp.zeros((seq, LANES), F32)
    b = jnp.zeros((seq, LANES), F32)
    for s0 in lane_starts:
        c = c.at[:, s0:s0 + half].set(cos).at[:, s0 + half:s0 + 2 * half].set(cos)
        a = a.at[:, s0:s0 + half].set(-sin)
        b = b.at[:, s0 + half:s0 + 2 * half].set(sin)
    return jnp.stack([c, a, b]) * scale


def _rope128(x, c, a, b, half):
    return x * c + pltpu.roll(x, LANES - half, 1) * a + pltpu.roll(x, half, 1) * b


def _rms(x, g):
    return x * lax.rsqrt(jnp.mean(x * x, axis=-1, keepdims=True) + RMS_EPS) * g


def _even_proj_kernel(x_ref, win_ref, wuq_ref, wk_ref, wv_ref, qg_ref, kvg_ref, t_ref,
                      qkva_ref, qb_ref, kb_ref, vb_ref):
    xb = x_ref[...].astype(BF16)
    h = jnp.dot(xb, win_ref[...], preferred_element_type=F32)
    ha = A_ROT // 2
    hb = B_ROPE // 2
    for hd in range(A_HEADS):
        lo = hd * LANES
        qkva_ref[:, lo:lo + LANES] = _rope128(h[:, lo:lo + LANES], t_ref[0], t_ref[1], t_ref[2], ha).astype(BF16)
        lo = A_W + hd * LANES
        qkva_ref[:, lo:lo + LANES] = _rope128(h[:, lo:lo + LANES], t_ref[3], t_ref[4], t_ref[5], ha).astype(BF16)
    qkva_ref[:, 2 * A_W:3 * A_W] = h[:, 2 * A_W:3 * A_W].astype(BF16)
    c0 = 3 * A_W
    cq = _rms(h[:, c0:c0 + B_Q_RANK], qg_ref[...])
    qb = jnp.dot(cq.astype(BF16), wuq_ref[...], preferred_element_type=F32)
    ckv = _rms(h[:, c0 + B_Q_RANK:c0 + B_Q_RANK + B_KV_RANK], kvg_ref[...]).astype(BF16)
    kn = jnp.dot(ckv, wk_ref[...], preferred_element_type=F32)
    vb_ref[...] = jnp.dot(ckv, wv_ref[...], preferred_element_type=F32).astype(BF16)
    kr = _rope128(h[:, EVEN_IN_PAD - LANES:], t_ref[9], t_ref[10], t_ref[11], hb)
    for hd in range(B_HEADS):
        lo = hd * LANES
        qb_ref[:, lo:lo + LANES] = _rope128(qb[:, lo:lo + LANES], t_ref[6], t_ref[7], t_ref[8], hb).astype(BF16)
        kb_ref[:, lo:lo + LANES] = (kn[:, lo:lo + LANES] + kr).astype(BF16)


def _even_proj(x2d, win, wuq, wk, wv, qg, kvg, tabs, seq):
    n = x2d.shape[0]
    tm = TOKEN_TILE
    assert n % tm == 0 and seq % tm == 0
    spb = seq // tm
    full = lambda shape: pl.BlockSpec(shape, lambda i: (0,) * len(shape))
    row = lambda w: pl.BlockSpec((tm, w), lambda i: (i, 0))
    return pl.pallas_call(
        _even_proj_kernel,
        grid=(n // tm,),
        in_specs=[row(D_MODEL), full(win.shape), full(wuq.shape), full(wk.shape), full(wv.shape),
                  full(qg.shape), full(kvg.shape),
                  pl.BlockSpec((12, tm, LANES), lambda i: (0, i % spb, 0))],
        out_specs=[row(3 * A_W), row(B_HEADS * LANES), row(B_HEADS * LANES), row(B_HEADS * B_V)],
        out_shape=[jax.ShapeDtypeStruct((n, 3 * A_W), BF16),
                   jax.ShapeDtypeStruct((n, B_HEADS * LANES), BF16),
                   jax.ShapeDtypeStruct((n, B_HEADS * LANES), BF16),
                   jax.ShapeDtypeStruct((n, B_HEADS * B_V), BF16)],
        compiler_params=_cparams(("parallel",)),
        name="even_proj",
    )(x2d, win, wuq, wk, wv, qg, kvg, tabs)


_NT = (((1,), (1,)), ((), ()))


def _softmax_pv(q, k, v, bias=None):
    s = lax.dot_general(q, k, _NT, preferred_element_type=F32)
    if bias is not None:
        s = s + bias
    m = jnp.max(s, axis=-1, keepdims=True)
    e = jnp.exp(s - m)
    l = jnp.sum(e, axis=-1, keepdims=True)
    o = jnp.dot(e.astype(BF16), v, preferred_element_type=F32)
    return o / l


def _diff_attn_kernel(lam_ref, q_ref, k_ref, v_ref, g_ref, o_ref, *, out_scale):
    q = q_ref[0]
    k = k_ref[0]
    v = v_ref[0]
    lane = lax.broadcasted_iota(jnp.int32, q.shape, 1)
    zero = jnp.zeros_like(q)
    o1 = _softmax_pv(jnp.where(lane < A_HEAD_DIM, q, zero), k, v)
    o2 = _softmax_pv(jnp.where(lane >= A_HEAD_DIM, q, zero), k, v)
    o = o1 - lam_ref[0] * o2
    o_ref[0] = (_rms(o, g_ref[...]) * out_scale).astype(BF16)


def _diff_attn(qkva, lam, subln_g, out_scale):
    b, s, _ = qkva.shape
    tq = ATTN_Q_TILE
    return pl.pallas_call(
        functools.partial(_diff_attn_kernel, out_scale=out_scale),
        grid=(b, A_HEADS, s // tq),
        in_specs=[pl.BlockSpec(memory_space=pltpu.SMEM),
                  pl.BlockSpec((1, tq, LANES), lambda bi, h, qi: (bi, qi, h)),
                  pl.BlockSpec((1, s, LANES), lambda bi, h, qi: (bi, 0, A_HEADS + h)),
                  pl.BlockSpec((1, s, LANES), lambda bi, h, qi: (bi, 0, 2 * A_HEADS + h)),
                  pl.BlockSpec((1, LANES), lambda bi, h, qi: (0, 0))],
        out_specs=pl.BlockSpec((1, tq, LANES), lambda bi, h, qi: (bi, qi, h)),
        out_shape=jax.ShapeDtypeStruct((b, s, A_W), BF16),
        compiler_params=_cparams(("parallel", "parallel", "parallel")),
        name="diff_attn",
    )(lam, qkva, qkva, qkva, subln_g)


def _mla_attn_kernel(q_ref, k_ref, v_ref, o_ref):
    v = v_ref[0]
    outs = []
    for j in range(2):
        q = q_ref[0, :, j * LANES:(j + 1) * LANES]
        k = k_ref[0, :, j * LANES:(j + 1) * LANES]
        outs.append(_softmax_pv(q, k, v))
    lane = lax.broadcasted_iota(jnp.int32, outs[0].shape, 1)
    o_ref[0] = jnp.where(lane < B_V, outs[0], outs[1]).astype(BF16)


def _mla_attn(qb, kb, vb):
    b, s, _ = qb.shape
    tq = ATTN_Q_TILE
    return pl.pallas_call(
        _mla_attn_kernel,
        grid=(b, B_HEADS // 2, s // tq),
        in_specs=[pl.BlockSpec((1, tq, 2 * LANES), lambda bi, p, qi: (bi, qi, p)),
                  pl.BlockSpec((1, s, 2 * LANES), lambda bi, p, qi: (bi, 0, p)),
                  pl.BlockSpec((1, s, LANES), lambda bi, p, qi: (bi, 0, p))],
        out_specs=pl.BlockSpec((1, tq, LANES), lambda bi, p, qi: (bi, qi, p)),
        out_shape=jax.ShapeDtypeStruct((b, s, B_HEADS * B_V), BF16),
        compiler_params=_cparams(("parallel", "parallel", "parallel")),
        name="mla_attn",
    )(qb, kb, vb)


def _proj_kernel(x_ref, w_ref, o_ref):
    o_ref[...] = jnp.dot(x_ref[...].astype(BF16), w_ref[...], preferred_element_type=F32).astype(o_ref.dtype)


def _proj(x2d, w):
    n, kdim = x2d.shape
    m = w.shape[1]
    tm = TOKEN_TILE
    return pl.pallas_call(
        _proj_kernel,
        grid=(n // tm,),
        in_specs=[pl.BlockSpec((tm, kdim), lambda i: (i, 0)), pl.BlockSpec((kdim, m), lambda i: (0, 0))],
        out_specs=pl.BlockSpec((tm, m), lambda i: (i, 0)),
        out_shape=jax.ShapeDtypeStruct((n, m), BF16),
        compiler_params=_cparams(("parallel",)),
        name="qkv_proj",
    )(x2d, w)


def _na_variant_blocks(rows):
    return (0, 1, rows // NA_Q_ROWS - 1)


def _na_band_start(i, rows):
    return jnp.clip(i - 1, 0, rows // NA_Q_ROWS - NA_BAND_CHUNKS)


def _natten_tables(rel_bias, rows):
    band_rows = NA_BAND_CHUNKS * NA_Q_ROWS
    heads = rel_bias.shape[0]
    qc = np.arange(GRID_W)
    c0 = np.clip(qc - NA_COLS // 2, 0, GRID_W - NA_COLS)
    col_in = (qc[None, :] >= c0[:, None]) & (qc[None, :] < c0[:, None] + NA_COLS)
    col_off = qc[None, :] - qc[:, None] + NA_COLS - 1
    csel = (col_in[:, :, None] & (col_off[:, :, None] == np.arange(2 * NA_COLS - 1))).astype(np.float32)
    tabs = []
    for i_rep in _na_variant_blocks(rows):
        bs = int(np.clip(i_rep - 1, 0, rows // NA_Q_ROWS - NA_BAND_CHUNKS)) * NA_Q_ROWS
        qr = NA_Q_ROWS * i_rep + np.arange(NA_Q_ROWS)
        kr = bs + np.arange(band_rows)
        r0 = np.clip(qr - NA_ROWS // 2, 0, rows - NA_ROWS)
        row_in = (kr[None, :] >= r0[:, None]) & (kr[None, :] < r0[:, None] + NA_ROWS)
        row_off = kr[None, :] - qr[:, None] + NA_ROWS - 1
        rsel = (row_in[:, :, None] & (row_off[:, :, None] == np.arange(2 * NA_ROWS - 1))).astype(np.float32)
        inwin = row_in[:, None, :, None] & col_in[None, :, None, :]
        u = jnp.einsum('hab,rsa->hrsb', rel_bias.astype(F32), rsel, precision=lax.Precision.HIGHEST)
        t = jnp.einsum('hrsb,cdb->hrcsd', u, csel, precision=lax.Precision.HIGHEST)
        t = jnp.where(inwin[None], t, NEG_BIG)
        tabs.append(t.reshape(heads, NA_Q_TILE, band_rows * GRID_W))
    return jnp.stack(tabs, axis=1)


def _natten_kernel(q_ref, k0_ref, k1_ref, k2_ref, v0_ref, v1_ref, v2_ref, t_ref, o_ref, *, bt):
    lane = lax.broadcasted_iota(jnp.int32, (NA_Q_TILE, LANES), 1)
    lo = lane < C_HEAD_DIM
    for b in range(bt):
        q = q_ref[b] * (C_HEAD_DIM ** -0.5)
        zero = jnp.zeros_like(q)
        k = jnp.concatenate([k0_ref[b], k1_ref[b], k2_ref[b]], axis=0)
        v = jnp.concatenate([v0_ref[b], v1_ref[b], v2_ref[b]], axis=0)
        o0 = _softmax_pv(jnp.where(lo, q, zero), k, v, t_ref[0])
        o1 = _softmax_pv(jnp.where(lo, zero, q), k, v, t_ref[1])
        o_ref[b] = jnp.where(lo, o0, o1).astype(BF16)


def _natten(qkv, tabs):
    b, s, _ = qkv.shape
    rows = s // GRID_W
    nblk = rows // NA_Q_ROWS
    bt = NA_BATCH_TILE if b % NA_BATCH_TILE == 0 else 1
    npair = C_HEADS // 2
    blk = (bt, NA_Q_TILE, LANES)

    def band_spec(col0, j):
        return pl.BlockSpec(blk, lambda p, i, bi: (bi, _na_band_start(i, rows) + j, col0 + p))

    def variant(i):
        return jnp.where(i == 0, 0, jnp.where(i == nblk - 1, 2, 1))

    return pl.pallas_call(
        functools.partial(_natten_kernel, bt=bt),
        grid=(npair, nblk, b // bt),
        in_specs=[pl.BlockSpec(blk, lambda p, i, bi: (bi, i, p))]
                 + [band_spec(npair, j) for j in range(NA_BAND_CHUNKS)]
                 + [band_spec(2 * npair, j) for j in range(NA_BAND_CHUNKS)]
                 + [pl.BlockSpec((2, None, NA_Q_TILE, NA_BAND_CHUNKS * NA_Q_TILE),
                                 lambda p, i, bi: (p, variant(i), 0, 0))],
        out_specs=pl.BlockSpec(blk, lambda p, i, bi: (bi, i, p)),
        out_shape=jax.ShapeDtypeStruct((b, s, C_WIDTH), BF16),
        compiler_params=_cparams(("parallel", "parallel", "parallel")),
        name="natten",
    )(qkv, qkv, qkv, qkv, qkv, qkv, qkv, tabs)


def _layer_norm(z, g, b):
    mu = jnp.mean(z, axis=-1, keepdims=True)
    zc = z - mu
    var = jnp.mean(zc * zc, axis=-1, keepdims=True)
    return zc * lax.rsqrt(var + LN_EPS) * g + b


def _route(logits):
    lane_i = lax.broadcasted_iota(jnp.int32, logits.shape, 1)
    lane = lane_i.astype(F32)
    big = float(LANES)
    gl = jnp.where(lane_i < N_GROUPS, logits, NEG_BIG)
    gmax = jnp.max(gl, axis=-1, keepdims=True)
    grp = jnp.min(jnp.where(gl == gmax, lane, big), axis=-1, keepdims=True)
    g_w = 1.0 / jnp.sum(jnp.exp(gl - gmax), axis=-1, keepdims=True)
    e_lo = grp * EXPERTS_PER_GROUP + N_GROUPS
    in_grp = jnp.logical_and(lane >= e_lo, lane < e_lo + EXPERTS_PER_GROUP)
    el = jnp.where(in_grp, logits, NEG_BIG)
    v1 = jnp.max(el, axis=-1, keepdims=True)
    i1 = jnp.min(jnp.where(el == v1, lane, big), axis=-1, keepdims=True)
    el2 = jnp.where(lane == i1, NEG_BIG, el)
    v2 = jnp.max(el2, axis=-1, keepdims=True)
    i2 = jnp.min(jnp.where(el2 == v2, lane, big), axis=-1, keepdims=True)
    ex = jnp.exp(v2 - v1)
    den = 1.0 + ex
    gate1 = (1.0 / den) * g_w
    gate2 = (ex / den) * g_w
    rec = jnp.where(lane_i == 0, i1 - N_GROUPS,
                    jnp.where(lane_i == 1, i2 - N_GROUPS,
                              jnp.where(lane_i == 2, gate1, jnp.where(lane_i == 3, gate2, 0.0))))
    return rec


def _post_attn_kernel(*refs, n_act):
    acts = refs[:n_act]
    ws = refs[n_act:2 * n_act]
    x_ref, g_ref, b_ref, wrh_ref, wrl_ref, br_ref, x1_ref, route_ref = refs[2 * n_act:]
    m = jnp.dot(acts[0][...], ws[0][...], preferred_element_type=F32)
    for a_ref, w_ref in zip(acts[1:], ws[1:]):
        m = m + jnp.dot(a_ref[...], w_ref[...], preferred_element_type=F32)
    y = _layer_norm(DN_ALPHA * x_ref[...] + m, g_ref[...], b_ref[...])
    x1_ref[...] = y
    y_hi = y.astype(BF16)
    y_lo = (y - y_hi.astype(F32)).astype(BF16)
    wrh = wrh_ref[...]
    logits = (jnp.dot(y_hi, wrh, preferred_element_type=F32)
              + jnp.dot(y_lo, wrh, preferred_element_type=F32)
              + jnp.dot(y_hi, wrl_ref[...], preferred_element_type=F32)) + br_ref[...]
    route_ref[...] = _route(logits)


def _post_attn(acts, ws, x2d, g, b, wrh, wrl, br):
    n = x2d.shape[0]
    tm = TOKEN_TILE
    full = lambda arr: pl.BlockSpec(arr.shape, lambda i: (0,) * arr.ndim)
    row = lambda w: pl.BlockSpec((tm, w), lambda i: (i, 0))
    return pl.pallas_call(
        functools.partial(_post_attn_kernel, n_act=len(acts)),
        grid=(n // tm,),
        in_specs=[row(a.shape[1]) for a in acts] + [full(w) for w in ws]
                 + [row(D_MODEL), full(g), full(b), full(wrh), full(wrl), full(br)],
        out_specs=[row(D_MODEL), row(ROUTE_LANES)],
        out_shape=[jax.ShapeDtypeStruct((n, D_MODEL), F32), jax.ShapeDtypeStruct((n, ROUTE_LANES), F32)],
        compiler_params=_cparams(("parallel",)),
        name="post_attn",
    )(*acts, *ws, x2d, g, b, wrh, wrl, br)


def _moe_plan(route, n_tok, bm):
    n_assign = n_tok * TOP_K
    nb = n_assign // bm + N_EXPERTS
    e_flat = route[:, :TOP_K].astype(jnp.int32).reshape(n_assign)
    order = jnp.argsort(e_flat, stable=True).astype(jnp.int32)
    counts = jnp.sum((e_flat[:, None] == jnp.arange(N_EXPERTS, dtype=jnp.int32)[None, :]).astype(jnp.int32), axis=0)
    start = jnp.cumsum(counts) - counts
    padded = (counts + bm - 1) // bm * bm
    pad_end = jnp.cumsum(padded)
    pad_start = pad_end - padded
    n_used = (pad_end[-1] // bm).astype(jnp.int32)
    blk = jnp.arange(nb, dtype=jnp.int32)
    bexp = jnp.sum((pad_end[None, :] <= (blk * bm)[:, None]).astype(jnp.int32), axis=1)
    bexp = jnp.minimum(bexp, N_EXPERTS - 1)
    used = blk < n_used
    bexp = jnp.where(used, bexp, bexp[jnp.maximum(n_used - 1, 0)])
    j = (blk * bm - pad_start[bexp])[:, None] + jnp.arange(bm, dtype=jnp.int32)[None, :]
    cnt = counts[bexp][:, None]
    valid = (j < cnt) & used[:, None]
    a = order[jnp.clip(start[bexp][:, None] + j, 0, n_assign - 1)]
    slot_tok = jnp.where(valid, a // TOP_K, 0)
    pads_before = (pad_start - start)[bexp][:, None]
    pad_dst = jnp.where(used[:, None], n_assign + pads_before + (j - cnt),
                        (blk * bm)[:, None] + jnp.arange(bm, dtype=jnp.int32)[None, :])
    slot_dst = jnp.where(valid, (a % TOP_K) * n_tok + a // TOP_K, pad_dst)
    return (bexp.astype(jnp.int32), n_used.reshape(1), slot_tok.astype(jnp.int32).reshape(nb, 1, bm),
            slot_dst.astype(jnp.int32).reshape(nb, 1, bm))


def _expert_kernel(bexp_ref, nused_ref, tokc_ref, tokn_ref, dst_ref, x_hbm, w1_ref, w3_ref, w2_ref,
                   out_hbm, xbuf, ybuf, zbuf, w13, w2b, gsem, ssem, zsem, *, bm):
    i = pl.program_id(0)
    nb = pl.num_programs(0)
    n_used = nused_ref[0]
    slot = i % 2

    def gather_rows(tok_ref, s):
        for r in range(bm):
            pltpu.make_async_copy(x_hbm.at[pl.ds(tok_ref[0, 0, r], 1)], xbuf.at[s, pl.ds(r, 1)], gsem.at[s]).start()

    def wait_gather(s):
        pltpu.make_async_copy(xbuf.at[s], xbuf.at[s], gsem.at[s]).wait()

    def wait_scatter(s):
        pltpu.make_async_copy(ybuf.at[s], ybuf.at[s], ssem.at[s]).wait()

    @pl.when(jnp.logical_and(i == 0, n_used > 0))
    def _():
        gather_rows(tokc_ref, 0)

    @pl.when(i + 1 < n_used)
    def _():
        gather_rows(tokn_ref, 1 - slot)

    @pl.when(i < n_used)
    def _():
        changed = jnp.logical_or(i == 0, bexp_ref[i] != bexp_ref[jnp.maximum(i - 1, 0)])

        @pl.when(changed)
        def _():
            w13[:, :D_EXPERT] = w1_ref[...].astype(BF16)
            w13[:, D_EXPERT:] = w3_ref[...].astype(BF16)
            w2b[...] = w2_ref[...].astype(BF16)

        wait_gather(slot)
        xb = xbuf[slot].astype(BF16)
        h = jnp.dot(xb, w13[...], preferred_element_type=F32)
        gate = h[:, :D_EXPERT]
        act = gate * (1.0 / (1.0 + jnp.exp(-gate))) * h[:, D_EXPERT:]
        y = jnp.dot(act.astype(BF16), w2b[...], preferred_element_type=F32)

        @pl.when(i >= 2)
        def _():
            wait_scatter(slot)

        ybuf[slot] = y

        for r in range(bm):
            pltpu.make_async_copy(ybuf.at[slot, pl.ds(r, 1)], out_hbm.at[pl.ds(dst_ref[0, 0, r], 1)], ssem.at[slot]).start()

    @pl.when(i == 0)
    def _():
        zbuf[...] = jnp.zeros_like(zbuf)

    @pl.when(i >= n_used)
    def _():
        row0 = pl.multiple_of(dst_ref[0, 0, 0], bm)
        cp = pltpu.make_async_copy(zbuf, out_hbm.at[pl.ds(row0, bm)], zsem)
        cp.start()
        cp.wait()

    @pl.when(i == nb - 1)
    def _():
        @pl.when(n_used >= 1)
        def _():
            wait_scatter((n_used - 1) % 2)

        @pl.when(n_used >= 2)
        def _():
            wait_scatter(n_used % 2)


def _experts(x1, plan, w_gate, w_up, w_down, layer):
    bexp, n_used, slot_tok, slot_dst = plan
    nb, _, bm = slot_tok.shape
    smem_blk = lambda fn: pl.BlockSpec((1, 1, bm), fn, memory_space=pltpu.SMEM)
    wspec = lambda k, m: pl.BlockSpec((None, None, k, m), lambda i, be, nu: (layer, be[i], 0, 0))
    grid_spec = pltpu.PrefetchScalarGridSpec(
        num_scalar_prefetch=2,
        grid=(nb,),
        in_specs=[smem_blk(lambda i, be, nu: (i, 0, 0)),
                  smem_blk(lambda i, be, nu: (jnp.minimum(i + 1, nb - 1), 0, 0)),
                  smem_blk(lambda i, be, nu: (i, 0, 0)),
                  pl.BlockSpec(memory_space=pl.ANY),
                  wspec(D_MODEL, D_EXPERT), wspec(D_MODEL, D_EXPERT), wspec(D_EXPERT, D_MODEL)],
        out_specs=pl.BlockSpec(memory_space=pl.ANY),
        scratch_shapes=[pltpu.VMEM((2, bm, D_MODEL), F32), pltpu.VMEM((2, bm, D_MODEL), F32),
                        pltpu.VMEM((bm, D_MODEL), F32),
                        pltpu.VMEM((D_MODEL, 2 * D_EXPERT), BF16), pltpu.VMEM((D_EXPERT, D_MODEL), BF16),
                        pltpu.SemaphoreType.DMA((2,)), pltpu.SemaphoreType.DMA((2,)), pltpu.SemaphoreType.DMA(())],
    )
    return pl.pallas_call(
        functools.partial(_expert_kernel, bm=bm),
        grid_spec=grid_spec,
        out_shape=jax.ShapeDtypeStruct((nb * bm, D_MODEL), F32),
        compiler_params=_cparams(("arbitrary",)),
        name="experts",
    )(bexp, n_used, slot_tok, slot_tok, slot_dst, x1, w_gate, w_up, w_down)


def _post_moe_kernel(x1_ref, y0_ref, y1_ref, route_ref, g_ref, b_ref, o_ref):
    r = route_ref[...]
    f = y0_ref[...] * r[:, 2:3] + y1_ref[...] * r[:, 3:4]
    o_ref[...] = _layer_norm(DN_ALPHA * x1_ref[...] + f, g_ref[...], b_ref[...])


def _post_moe(x1, y, route, g, b):
    n = x1.shape[0]
    tm = TOKEN_TILE
    nt = n // tm
    full = lambda arr: pl.BlockSpec(arr.shape, lambda i: (0,) * arr.ndim)
    return pl.pallas_call(
        _post_moe_kernel,
        grid=(nt,),
        in_specs=[pl.BlockSpec((tm, D_MODEL), lambda i: (i, 0)),
                  pl.BlockSpec((tm, D_MODEL), lambda i: (i, 0)),
                  pl.BlockSpec((tm, D_MODEL), lambda i: (nt + i, 0)),
                  pl.BlockSpec((tm, ROUTE_LANES), lambda i: (i, 0)),
                  full(g), full(b)],
        out_specs=pl.BlockSpec((tm, D_MODEL), lambda i: (i, 0)),
        out_shape=jax.ShapeDtypeStruct((n, D_MODEL), F32),
        compiler_params=_cparams(("parallel",)),
        name="post_moe",
    )(x1, y, y, route, g, b)


def _prep_even(w_in, w_uq, w_ukv, w_o):
    c0 = 3 * A_W + B_Q_RANK + B_KV_RANK
    win = jnp.zeros((D_MODEL, EVEN_IN_PAD), F32)
    win = win.at[:, :c0].set(w_in[:, :c0])
    kr0 = EVEN_IN_PAD - LANES + KR_LANE0
    win = win.at[:, kr0:kr0 + B_ROPE].set(w_in[:, c0:c0 + B_ROPE])
    wuq = w_uq.reshape(B_Q_RANK, B_HEADS, B_NOPE + B_ROPE)
    wuq = jnp.pad(wuq, ((0, 0), (0, 0), (0, LANES - B_NOPE - B_ROPE))).reshape(B_Q_RANK, B_HEADS * LANES)
    wukv = w_ukv.reshape(B_KV_RANK, B_HEADS, B_NOPE + B_V)
    wk = jnp.pad(wukv[:, :, :B_NOPE], ((0, 0), (0, 0), (0, LANES - B_NOPE))).reshape(B_KV_RANK, B_HEADS * LANES)
    wv = wukv[:, :, B_NOPE:].reshape(B_KV_RANK, B_HEADS * B_V)
    return (win.astype(BF16), wuq.astype(BF16), wk.astype(BF16), wv.astype(BF16),
            w_o[:A_W].astype(BF16), w_o[A_W:].astype(BF16))


def _prep_router(w_rg, b_rg, w_re, b_re):
    wr = jnp.zeros((D_MODEL, ROUTE_LANES), F32)
    wr = wr.at[:, :N_GROUPS].set(w_rg).at[:, N_GROUPS:N_GROUPS + N_EXPERTS].set(w_re)
    br = jnp.zeros((1, ROUTE_LANES), F32)
    br = br.at[0, :N_GROUPS].set(b_rg).at[0, N_GROUPS:N_GROUPS + N_EXPERTS].set(b_re)
    wr_hi = wr.astype(BF16)
    wr_lo = (wr - wr_hi.astype(F32)).astype(BF16)
    return wr_hi, wr_lo, br


def kernel(x, even_w_in, even_lam_q1, even_lam_k1, even_lam_q2, even_lam_k2, even_subln_g, even_q_norm_g, even_w_uq, even_kv_norm_g, even_w_ukv, even_w_o, odd_w_qkv, odd_rel_bias, odd_w_o, ln1_g, ln1_b, ln2_g, ln2_b, w_router_group, b_router_group, w_router_expert, b_router_expert, w_gate, w_up, w_down):
    bsz, seq, d = x.shape
    n = bsz * seq
    rows = seq // GRID_W
    x2d = x.reshape(n, d)

    a_scale = A_HEAD_DIM ** -0.5
    b_scale = (B_NOPE + B_ROPE) ** -0.5
    a_lanes = (0, A_HEAD_DIM)
    rope_tabs = jnp.concatenate([
        _rope_block_tables(seq, A_ROT, a_lanes, a_scale),
        _rope_block_tables(seq, A_ROT, a_lanes, 1.0),
        _rope_block_tables(seq, B_ROPE, (B_NOPE,), b_scale),
        _rope_block_tables(seq, B_ROPE, (KR_LANE0,), 1.0)], axis=0)

    for i in range(DEPTH):
        j = i // 2
        if i % 2 == 0:
            lambda_init = 0.8 - 0.6 * math.exp(-0.3 * i)
            win, wuq, wk, wv, woa, wob = _prep_even(even_w_in[j], even_w_uq[j], even_w_ukv[j], even_w_o[j])
            lam = (jnp.exp(jnp.sum(even_lam_q1[j] * even_lam_k1[j]))
                   - jnp.exp(jnp.sum(even_lam_q2[j] * even_lam_k2[j])) + lambda_init).reshape(1).astype(F32)
            qkva, qb, kb, vb = _even_proj(x2d, win, wuq, wk, wv, even_q_norm_g[j].reshape(1, -1),
                                          even_kv_norm_g[j].reshape(1, -1), rope_tabs, seq)
            a_out = _diff_attn(qkva.reshape(bsz, seq, -1), lam, even_subln_g[j].reshape(1, -1), 1.0 - lambda_init)
            b_out = _mla_attn(qb.reshape(bsz, seq, -1), kb.reshape(bsz, seq, -1), vb.reshape(bsz, seq, -1))
            acts = [a_out.reshape(n, -1), b_out.reshape(n, -1)]
            ws = [woa, wob]
        else:
            qkv = _proj(x2d, odd_w_qkv[j].astype(BF16))
            o = _natten(qkv.reshape(bsz, seq, -1), _natten_tables(odd_rel_bias[j], rows))
            acts = [o.reshape(n, -1)]
            ws = [odd_w_o[j].astype(BF16)]
        wr_hi, wr_lo, br = _prep_router(w_router_group[i], b_router_group[i], w_router_expert[i], b_router_expert[i])
        x1, route = _post_attn(acts, ws, x2d, ln1_g[i].reshape(1, -1), ln1_b[i].reshape(1, -1), wr_hi, wr_lo, br)
        plan = _moe_plan(route, n, MOE_BLOCK)
        y = _experts(x1, plan, w_gate, w_up, w_down, i)
        x2d = _post_moe(x1, y, route,ln2_g[i].reshape(1, -1), ln2_b[i].reshape(1, -1))
    return x2d.reshape(bsz, seq, d)
```

```python
import functools
import math

import numpy as np
import jax
import jax.numpy as jnp
from jax import lax
from jax.experimental import pallas as pl
from jax.experimental.pallas import tpu as pltpu

F32 = jnp.float32
BF16 = jnp.bfloat16

D_MODEL = 1024
DEPTH = 4
GRID_W = 64
ROPE_THETA = 500000.0
A_HEADS = 4
A_HEAD_DIM = 64
A_ROT = A_HEAD_DIM // 4
A_W = A_HEADS * 2 * A_HEAD_DIM
B_HEADS = 8
B_NOPE = 64
B_ROPE = 32
B_V = 64
B_Q_RANK = 256
B_KV_RANK = 128
C_HEADS = 16
C_HEAD_DIM = 64
C_WIDTH = C_HEADS * C_HEAD_DIM
NA_ROWS = 8
NA_COLS = 16
N_GROUPS = 4
EXPERTS_PER_GROUP = 8
N_EXPERTS = N_GROUPS * EXPERTS_PER_GROUP
TOP_K = 2
D_EXPERT = 512
DN_ALPHA = (2 * DEPTH) ** 0.25
LN_EPS = 1e-5
RMS_EPS = 1e-6

LANES = 128
VMEM_LIMIT_BYTES = 52 * 1024 * 1024

EVEN_IN_PAD = 2048
KR_LANE0 = 64
TOKEN_TILE = 512
ATTN_Q_TILE = 1024
NA_Q_ROWS = 4
NA_Q_TILE = NA_Q_ROWS * GRID_W
NA_BAND_CHUNKS = 3
NA_BATCH_TILE = 4
MOE_BLOCK = 256
NEG_BIG = -1e30
LOG2E = math.log2(math.e)
ROUTE_LANES = LANES


def _cparams(sem):
    return pltpu.CompilerParams(dimension_semantics=sem, vmem_limit_bytes=VMEM_LIMIT_BYTES)


def _rope_block_tables(seq, rot_dim, lane_starts, scale):
    half = rot_dim // 2
    inv_freq = ROPE_THETA ** (-jnp.arange(0, rot_dim, 2, dtype=F32) / rot_dim)
    ang = jnp.arange(seq, dtype=F32)[:, None] * inv_freq[None, :]
    cos, sin = jnp.cos(ang), jnp.sin(ang)
    c = jnp.ones((seq, LANES), F32)
    a = jnp.zeros((seq, LANES), F32)
    b = jnp.zeros((seq, LANES), F32)
    for s0 in lane_starts:
        c = c.at[:, s0:s0 + half].set(cos).at[:, s0 + half:s0 + 2 * half].set(cos)
        a = a.at[:, s0:s0 + half].set(-sin)
        b = b.at[:, s0 + half:s0 + 2 * half].set(sin)
    return jnp.stack([c, a, b]) * scale


def _rope128(x, c, a, b, half):
    return x * c + pltpu.roll(x, LANES - half, 1) * a + pltpu.roll(x, half, 1) * b


def _rms(x, g):
    return x * lax.rsqrt(jnp.mean(x * x, axis=-1, keepdims=True) + RMS_EPS) * g


def _even_proj_kernel(x_ref, win_ref, wuq_ref, wk_ref, wv_ref, qg_ref, kvg_ref, t_ref,
                      qkva_ref, qb_ref, kb_ref, vb_ref):
    xb = x_ref[...].astype(BF16)
    h = jnp.dot(xb, win_ref[...], preferred_element_type=F32)
    ha = A_ROT // 2
    hb = B_ROPE // 2
    for hd in range(A_HEADS):
        lo = hd * LANES
        qkva_ref[hd] = _rope128(h[:, lo:lo + LANES], t_ref[0], t_ref[1], t_ref[2], ha).astype(BF16)
        lo = A_W + hd * LANES
        qkva_ref[A_HEADS + hd] = _rope128(h[:, lo:lo + LANES], t_ref[3], t_ref[4], t_ref[5], ha).astype(BF16)
        lo = 2 * A_W + hd * LANES
        qkva_ref[2 * A_HEADS + hd] = h[:, lo:lo + LANES].astype(BF16)
    c0 = 3 * A_W
    cq = _rms(h[:, c0:c0 + B_Q_RANK], qg_ref[...])
    qb = jnp.dot(cq.astype(BF16), wuq_ref[...], preferred_element_type=F32)
    ckv = _rms(h[:, c0 + B_Q_RANK:c0 + B_Q_RANK + B_KV_RANK], kvg_ref[...]).astype(BF16)
    kn = jnp.dot(ckv, wk_ref[...], preferred_element_type=F32)
    vb = jnp.dot(ckv, wv_ref[...], preferred_element_type=F32).astype(BF16)
    for p in range(B_HEADS // 2):
        vb_ref[p] = vb[:, p * LANES:(p + 1) * LANES]
    kr = _rope128(h[:, EVEN_IN_PAD - LANES:], t_ref[9], t_ref[10], t_ref[11], hb)
    for hd in range(B_HEADS):
        lo = hd * LANES
        qb_ref[hd] = _rope128(qb[:, lo:lo + LANES], t_ref[6], t_ref[7], t_ref[8], hb).astype(BF16)
        kb_ref[hd] = (kn[:, lo:lo + LANES] + kr).astype(BF16)


def _even_proj(x2d, win, wuq, wk, wv, qg, kvg, tabs, seq):
    n = x2d.shape[0]
    tm = TOKEN_TILE
    assert n % tm == 0 and seq % tm == 0
    spb = seq // tm
    full = lambda shape: pl.BlockSpec(shape, lambda i: (0,) * len(shape))
    row = lambda w: pl.BlockSpec((tm, w), lambda i: (i, 0))
    chunks = lambda c: pl.BlockSpec((c, tm, LANES), lambda i: (0, i, 0))
    return pl.pallas_call(
        _even_proj_kernel,
        grid=(n // tm,),
        in_specs=[row(D_MODEL), full(win.shape), full(wuq.shape), full(wk.shape), full(wv.shape),
                  full(qg.shape), full(kvg.shape),
                  pl.BlockSpec((12, tm, LANES), lambda i: (0, i % spb, 0))],
        out_specs=[chunks(3 * A_HEADS), chunks(B_HEADS), chunks(B_HEADS), chunks(B_HEADS // 2)],
        out_shape=[jax.ShapeDtypeStruct((3 * A_HEADS, n, LANES), BF16),
                   jax.ShapeDtypeStruct((B_HEADS, n, LANES), BF16),
                   jax.ShapeDtypeStruct((B_HEADS, n, LANES), BF16),
                   jax.ShapeDtypeStruct((B_HEADS // 2, n, LANES), BF16)],
        compiler_params=_cparams(("parallel",)),
        name="even_proj",
    )(x2d, win, wuq, wk, wv, qg, kvg, tabs)


_NT = (((1,), (1,)), ((), ()))


def _softmax_pv(q, k, v_ones, bias=None):
    s = lax.dot_general(q, k, _NT, preferred_element_type=F32)
    if bias is not None:
        s = s + bias
    m = jnp.max(s, axis=-1, keepdims=True)
    e = jnp.exp2(s - m).astype(BF16)
    ol = jnp.dot(e, v_ones, preferred_element_type=F32)
    return ol[:, :LANES] / ol[:, LANES:]


def _fill_v_ones(vext_ref, v):
    vext_ref[:, :LANES] = v
    vext_ref[:, LANES:] = jnp.ones(v.shape, v.dtype)


def _diff_attn_kernel(lam_ref, q_ref, k_ref, v_ref, g_ref, o_ref, vext_ref, *, out_scale):
    @pl.when(pl.program_id(2) == 0)
    def _():
        _fill_v_ones(vext_ref, v_ref[0])

    q = q_ref[0]
    k = k_ref[0]
    v_ones = vext_ref[...]
    lane = lax.broadcasted_iota(jnp.int32, q.shape, 1)
    zero = jnp.zeros_like(q)
    o1 = _softmax_pv(jnp.where(lane < A_HEAD_DIM, q, zero), k, v_ones)
    o2 = _softmax_pv(jnp.where(lane >= A_HEAD_DIM, q, zero), k, v_ones)
    o = o1 - lam_ref[0] * o2
    o_ref[0] = (_rms(o, g_ref[...]) * out_scale).astype(BF16)


def _diff_attn(qkva, lam, subln_g, out_scale):
    _, b, s, _ = qkva.shape
    tq = ATTN_Q_TILE
    blk = lambda rows: (None, 1, rows, LANES)
    return pl.pallas_call(
        functools.partial(_diff_attn_kernel, out_scale=out_scale),
        grid=(b, A_HEADS, s // tq),
        in_specs=[pl.BlockSpec(memory_space=pltpu.SMEM),
                  pl.BlockSpec(blk(tq), lambda bi, h, qi: (h, bi, qi, 0)),
                  pl.BlockSpec(blk(s), lambda bi, h, qi: (A_HEADS + h, bi, 0, 0)),
                  pl.BlockSpec(blk(s), lambda bi, h, qi: (2 * A_HEADS + h, bi, 0, 0)),
                  pl.BlockSpec((1, LANES), lambda bi, h, qi: (0, 0))],
        out_specs=pl.BlockSpec(blk(tq), lambda bi, h, qi: (h, bi, qi, 0)),
        out_shape=jax.ShapeDtypeStruct((A_HEADS, b, s, LANES), BF16),
        scratch_shapes=[pltpu.VMEM((s, 2 * LANES), BF16)],
        compiler_params=_cparams(("parallel", "parallel", "arbitrary")),
        name="diff_attn",
    )(lam, qkva, qkva, qkva, subln_g)


def _mla_attn_kernel(q_ref, k_ref, v_ref, o_ref, vext_ref):
    @pl.when(pl.program_id(2) == 0)
    def _():
        _fill_v_ones(vext_ref, v_ref[0])

    v_ones = vext_ref[...]
    outs = []
    for j in range(2):
        outs.append(_softmax_pv(q_ref[j, 0], k_ref[j, 0], v_ones))
    lane = lax.broadcasted_iota(jnp.int32, outs[0].shape, 1)
    o_ref[0] = jnp.where(lane < B_V, outs[0], outs[1]).astype(BF16)


def _mla_attn(qb, kb, vb):
    _, b, s, _ = qb.shape
    tq = ATTN_Q_TILE
    return pl.pallas_call(
        _mla_attn_kernel,
        grid=(b, B_HEADS // 2, s // tq),
        in_specs=[pl.BlockSpec((2, 1, tq, LANES), lambda bi, p, qi: (p, bi, qi, 0)),
                  pl.BlockSpec((2, 1, s, LANES), lambda bi, p, qi: (p, bi, 0, 0)),
                  pl.BlockSpec((None, 1, s, LANES), lambda bi, p, qi: (p, bi, 0, 0))],
        out_specs=pl.BlockSpec((None, 1, tq, LANES), lambda bi, p, qi: (p, bi, qi, 0)),
        out_shape=jax.ShapeDtypeStruct((B_HEADS // 2, b, s, LANES), BF16),
        scratch_shapes=[pltpu.VMEM((s, 2 * LANES), BF16)],
        compiler_params=_cparams(("parallel", "parallel", "arbitrary")),
        name="mla_attn",
    )(qb, kb, vb)


def _proj_kernel(x_ref, w_ref, o_ref, *, q_cols, q_scale):
    acc = jnp.dot(x_ref[...].astype(BF16), w_ref[...], preferred_element_type=F32)
    for c in range(o_ref.shape[0]):
        piece = acc[:, c * LANES:(c + 1) * LANES]
        o_ref[c] = (piece * q_scale if c * LANES < q_cols else piece).astype(o_ref.dtype)


def _proj(x2d, w, q_cols, q_scale):
    n, kdim = x2d.shape
    m = w.shape[1]
    tm = TOKEN_TILE
    return pl.pallas_call(
        functools.partial(_proj_kernel, q_cols=q_cols, q_scale=q_scale),
        grid=(n // tm,),
        in_specs=[pl.BlockSpec((tm, kdim), lambda i: (i, 0)), pl.BlockSpec((kdim, m), lambda i: (0, 0))],
        out_specs=pl.BlockSpec((m // LANES, tm, LANES), lambda i: (0, i, 0)),
        out_shape=jax.ShapeDtypeStruct((m // LANES, n, LANES), BF16),
        compiler_params=_cparams(("parallel",)),
        name="qkv_proj",
    )(x2d, w)


def _na_variant_blocks(rows):
    return (0, 1, rows // NA_Q_ROWS - 1)


def _na_band_start(i, rows):
    return jnp.clip(i - 1, 0, rows // NA_Q_ROWS - NA_BAND_CHUNKS)


def _natten_tables(rel_bias, rows):
    band_rows = NA_BAND_CHUNKS * NA_Q_ROWS
    heads = rel_bias.shape[0]
    qc = np.arange(GRID_W)
    c0 = np.clip(qc - NA_COLS // 2, 0, GRID_W - NA_COLS)
    col_in = (qc[None, :] >= c0[:, None]) & (qc[None, :] < c0[:, None] + NA_COLS)
    col_off = qc[None, :] - qc[:, None] + NA_COLS - 1
    csel = (col_in[:, :, None] & (col_off[:, :, None] == np.arange(2 * NA_COLS - 1))).astype(np.float32)
    tabs = []
    for i_rep in _na_variant_blocks(rows):
        bs = int(np.clip(i_rep - 1, 0, rows // NA_Q_ROWS - NA_BAND_CHUNKS)) * NA_Q_ROWS
        qr = NA_Q_ROWS * i_rep + np.arange(NA_Q_ROWS)
        kr = bs + np.arange(band_rows)
        r0 = np.clip(qr - NA_ROWS // 2, 0, rows - NA_ROWS)
        row_in = (kr[None, :] >= r0[:, None]) & (kr[None, :] < r0[:, None] + NA_ROWS)
        row_off = kr[None, :] - qr[:, None] + NA_ROWS - 1
        rsel = (row_in[:, :, None] & (row_off[:, :, None] == np.arange(2 * NA_ROWS - 1))).astype(np.float32)
        inwin = row_in[:, None, :, None] & col_in[None, :, None, :]
        u = jnp.einsum('hab,rsa->hrsb', rel_bias.astype(F32), rsel, precision=lax.Precision.HIGHEST)
        t = jnp.einsum('hrsb,cdb->hrcsd', u, csel, precision=lax.Precision.HIGHEST)
        t = jnp.where(inwin[None], t, NEG_BIG)
        tabs.append(t.reshape(heads, NA_Q_TILE, band_rows * GRID_W))
    return jnp.stack(tabs, axis=1)


def _natten_kernel(q_ref, k0_ref, k1_ref, k2_ref, v0_ref, v1_ref, v2_ref, t_ref, o_ref, *, bt):
    lane = lax.broadcasted_iota(jnp.int32, (NA_Q_TILE, LANES), 1)
    lo = lane < C_HEAD_DIM
    for b in range(bt):
        q = q_ref[b]
        zero = jnp.zeros_like(q)
        k = jnp.concatenate([k0_ref[b], k1_ref[b], k2_ref[b]], axis=0)
        v = jnp.concatenate([v0_ref[b], v1_ref[b], v2_ref[b]], axis=0)
        v_ones = jnp.concatenate([v, jnp.ones(v.shape, v.dtype)], axis=1)
        o0 = _softmax_pv(jnp.where(lo, q, zero), k, v_ones, t_ref[0])
        o1 = _softmax_pv(jnp.where(lo, zero, q), k, v_ones, t_ref[1])
        o_ref[b] = jnp.where(lo, o0, o1).astype(BF16)


def _natten(qkv, tabs):
    _, b, s, _ = qkv.shape
    rows = s // GRID_W
    nblk = rows // NA_Q_ROWS
    bt = NA_BATCH_TILE if b % NA_BATCH_TILE == 0 else 1
    npair = C_HEADS // 2
    blk = (None, bt, NA_Q_TILE, LANES)

    def band_spec(col0, j):
        return pl.BlockSpec(blk, lambda p, i, bi: (col0 + p, bi, _na_band_start(i, rows) + j, 0))

    def variant(i):
        return jnp.where(i == 0, 0, jnp.where(i == nblk - 1, 2, 1))

    return pl.pallas_call(
        functools.partial(_natten_kernel, bt=bt),
        grid=(npair, nblk, b // bt),
        in_specs=[pl.BlockSpec(blk, lambda p, i, bi: (p, bi, i, 0))]
                 + [band_spec(npair, j) for j in range(NA_BAND_CHUNKS)]
                 + [band_spec(2 * npair, j) for j in range(NA_BAND_CHUNKS)]
                 + [pl.BlockSpec((2, None, NA_Q_TILE, NA_BAND_CHUNKS * NA_Q_TILE),
                                 lambda p, i, bi: (p, variant(i), 0, 0))],
        out_specs=pl.BlockSpec(blk, lambda p, i, bi: (p, bi, i, 0)),
        out_shape=jax.ShapeDtypeStruct((npair, b, s, LANES), BF16),
        compiler_params=_cparams(("parallel", "parallel", "parallel")),
        name="natten",
    )(qkv, qkv, qkv, qkv, qkv, qkv, qkv, tabs)


def _layer_norm(z, g, b):
    mu = jnp.mean(z, axis=-1, keepdims=True)
    zc = z - mu
    var = jnp.mean(zc * zc, axis=-1, keepdims=True)
    return zc * lax.rsqrt(var + LN_EPS) * g + b


def _route(logits):
    lane_i = lax.broadcasted_iota(jnp.int32, logits.shape, 1)
    lane = lane_i.astype(F32)
    big = float(LANES)
    gl = jnp.where(lane_i < N_GROUPS, logits, NEG_BIG)
    gmax = jnp.max(gl, axis=-1, keepdims=True)
    grp = jnp.min(jnp.where(gl == gmax, lane, big), axis=-1, keepdims=True)
    g_w = 1.0 / jnp.sum(jnp.exp(gl - gmax), axis=-1, keepdims=True)
    e_lo = grp * EXPERTS_PER_GROUP + N_GROUPS
    in_grp = jnp.logical_and(lane >= e_lo, lane < e_lo + EXPERTS_PER_GROUP)
    el = jnp.where(in_grp, logits, NEG_BIG)
    v1 = jnp.max(el, axis=-1, keepdims=True)
    i1 = jnp.min(jnp.where(el == v1, lane, big), axis=-1, keepdims=True)
    el2 = jnp.where(lane == i1, NEG_BIG, el)
    v2 = jnp.max(el2, axis=-1, keepdims=True)
    i2 = jnp.min(jnp.where(el2 == v2, lane, big), axis=-1, keepdims=True)
    ex = jnp.exp(v2 - v1)
    den = 1.0 + ex
    gate1 = (1.0 / den) * g_w
    gate2 = (ex / den) * g_w
    rec = jnp.where(lane_i == 0, i1 - N_GROUPS,
                    jnp.where(lane_i == 1, i2 - N_GROUPS,
                              jnp.where(lane_i == 2, gate1, jnp.where(lane_i == 3, gate2, 0.0))))
    return rec


def _post_attn_kernel(*refs, n_act):
    acts = refs[:n_act]
    w_ref, x_ref, g_ref, b_ref, wrh_ref, wrl_ref, br_ref, x1_ref, route_ref = refs[n_act:]
    a = jnp.concatenate([a_ref[c] for a_ref in acts for c in range(a_ref.shape[0])], axis=1)
    m = jnp.dot(a, w_ref[...], preferred_element_type=F32)
    y = _layer_norm(DN_ALPHA * x_ref[...] + m, g_ref[...], b_ref[...])
    x1_ref[...] = y
    y_hi = y.astype(BF16)
    y_lo = (y - y_hi.astype(F32)).astype(BF16)
    wrh = wrh_ref[...]
    logits = (jnp.dot(y_hi, wrh, preferred_element_type=F32)
              + jnp.dot(y_lo, wrh, preferred_element_type=F32)
              + jnp.dot(y_hi, wrl_ref[...], preferred_element_type=F32)) + br_ref[...]
    route_ref[...] = _route(logits)


def _post_attn(acts, w_o, x2d, g, b, wrh, wrl, br):
    n = x2d.shape[0]
    tm = TOKEN_TILE
    full = lambda arr: pl.BlockSpec(arr.shape, lambda i: (0,) * arr.ndim)
    row = lambda w: pl.BlockSpec((tm, w), lambda i: (i, 0))
    return pl.pallas_call(
        functools.partial(_post_attn_kernel, n_act=len(acts)),
        grid=(n // tm,),
        in_specs=[pl.BlockSpec((a.shape[0], tm, LANES), lambda i: (0, i, 0)) for a in acts]
                 + [full(w_o), row(D_MODEL), full(g), full(b), full(wrh), full(wrl), full(br)],
        out_specs=[row(D_MODEL), row(ROUTE_LANES)],
        out_shape=[jax.ShapeDtypeStruct((n, D_MODEL), F32), jax.ShapeDtypeStruct((n, ROUTE_LANES), F32)],
        compiler_params=_cparams(("parallel",)),
        name="post_attn",
    )(*acts, w_o, x2d, g, b, wrh, wrl, br)


def _moe_plan(route, n_tok, bm):
    n_assign = n_tok * TOP_K
    nb = n_assign // bm + N_EXPERTS
    e_flat = route[:, :TOP_K].astype(jnp.int32).reshape(n_assign)
    order = jnp.argsort(e_flat, stable=True).astype(jnp.int32)
    counts = jnp.sum((e_flat[:, None] == jnp.arange(N_EXPERTS, dtype=jnp.int32)[None, :]).astype(jnp.int32), axis=0)
    start = jnp.cumsum(counts) - counts
    padded = (counts + bm - 1) // bm * bm
    pad_end = jnp.cumsum(padded)
    pad_start = pad_end - padded
    n_used = (pad_end[-1] // bm).astype(jnp.int32)
    blk = jnp.arange(nb, dtype=jnp.int32)
    bexp = jnp.sum((pad_end[None, :] <= (blk * bm)[:, None]).astype(jnp.int32), axis=1)
    bexp = jnp.minimum(bexp, N_EXPERTS - 1)
    used = blk < n_used
    bexp = jnp.where(used, bexp, bexp[jnp.maximum(n_used - 1, 0)])
    j = (blk * bm - pad_start[bexp])[:, None] + jnp.arange(bm, dtype=jnp.int32)[None, :]
    cnt = counts[bexp][:, None]
    valid = (j < cnt) & used[:, None]
    a = order[jnp.clip(start[bexp][:, None] + j, 0, n_assign - 1)]
    slot_tok = jnp.where(valid, a // TOP_K, 0)
    pads_before = (pad_start - start)[bexp][:, None]
    pad_dst = jnp.where(used[:, None], n_assign + pads_before + (j - cnt),
                        (blk * bm)[:, None] + jnp.arange(bm, dtype=jnp.int32)[None, :])
    slot_dst = jnp.where(valid, (a % TOP_K) * n_tok + a // TOP_K, pad_dst)
    return (bexp.astype(jnp.int32), n_used.reshape(1), slot_tok.astype(jnp.int32).reshape(nb, 1, bm),
            slot_dst.astype(jnp.int32).reshape(nb, 1, bm))


def _expert_kernel(bexp_ref, nused_ref, tok0_ref, tokn_ref, dstp_ref, x_hbm, w1_ref, w3_ref, w2_ref,
                   out_hbm, xbuf, ybuf, zbuf, w13, w2b, gsem, ssem, zsem, *, bm, nb):
    i = pl.program_id(0)
    n_used = nused_ref[0]
    slot = i % 2
    half = bm // 2

    def gather_rows(tok_ref, s, r0, r1):
        for r in range(r0, r1):
            pltpu.make_async_copy(x_hbm.at[pl.ds(tok_ref[0, 0, r], 1)], xbuf.at[s, pl.ds(r, 1)], gsem.at[s]).start()

    def scatter_rows(s, r0, r1):
        for r in range(r0, r1):
            pltpu.make_async_copy(ybuf.at[s, pl.ds(r, 1)], out_hbm.at[pl.ds(dstp_ref[0, 0, r], 1)], ssem.at[s]).start()

    def wait_gather(s):
        pltpu.make_async_copy(xbuf.at[s], xbuf.at[s], gsem.at[s]).wait()

    def wait_scatter(s):
        pltpu.make_async_copy(ybuf.at[s], ybuf.at[s], ssem.at[s]).wait()

    @pl.when(i == 0)
    def _():
        zbuf[...] = jnp.zeros_like(zbuf)
        ybuf[1] = jnp.zeros((bm, D_MODEL), F32)
        gather_rows(tok0_ref, 0, 0, bm)

    @pl.when(i < n_used)
    def _():
        changed = jnp.logical_or(i == 0, bexp_ref[i] != bexp_ref[jnp.maximum(i - 1, 0)])

        @pl.when(changed)
        def _():
            w13[:, :D_EXPERT] = w1_ref[...].astype(BF16)
            w13[:, D_EXPERT:] = w3_ref[...].astype(BF16)
            w2b[...] = w2_ref[...].astype(BF16)

    def block_step(s):
        wait_gather(s)
        xb = xbuf[s].astype(BF16)
        gate = jnp.dot(xb, w13[:, :D_EXPERT], preferred_element_type=F32)
        gather_rows(tokn_ref, 1 - s, 0, half)
        up = jnp.dot(xb, w13[:, D_EXPERT:], preferred_element_type=F32)
        gather_rows(tokn_ref, 1 - s, half, bm)
        act = (gate * (1.0 / (1.0 + jnp.exp(-gate))) * up).astype(BF16)
        y_lo = jnp.dot(act, w2b[:, :D_MODEL // 2], preferred_element_type=F32)
        scatter_rows(1 - s, 0, half)
        y_hi = jnp.dot(act, w2b[:, D_MODEL // 2:], preferred_element_type=F32)
        scatter_rows(1 - s, half, bm)

        @pl.when(i >= 1)
        def _():
            wait_scatter(s)

        ybuf[s, :, :D_MODEL // 2] = y_lo
        ybuf[s, :, D_MODEL // 2:] = y_hi

    for s in range(2):
        pl.when(jnp.logical_and(i < n_used, slot == s))(functools.partial(block_step, s))

    @pl.when(i == n_used)
    def _():
        wait_gather(slot)
        scatter_rows(1 - slot, 0, bm)
        wait_scatter(0)
        wait_scatter(1)

    @pl.when(jnp.logical_and(i >= n_used, i < nb))
    def _():
        cp = pltpu.make_async_copy(zbuf, out_hbm.at[pl.ds(pl.multiple_of(i * bm, bm), bm)], zsem)
        cp.start()
        cp.wait()


def _experts(x1, plan, w_gate, w_up, w_down, layer):
    bexp, n_used, slot_tok, slot_dst = plan
    nb, _, bm = slot_tok.shape
    spare = (nb * bm + jnp.arange(bm, dtype=jnp.int32)).reshape(1, 1, bm)
    dst_ext = jnp.concatenate([slot_dst, spare], axis=0)
    smem_blk = lambda fn: pl.BlockSpec((1, 1, bm), fn, memory_space=pltpu.SMEM)
    wspec = lambda k, m: pl.BlockSpec((None, None, k, m),
                                      lambda i, be, nu: (layer, be[jnp.minimum(i, nb - 1)], 0, 0))
    grid_spec = pltpu.PrefetchScalarGridSpec(
        num_scalar_prefetch=2,
        grid=(nb + 1,),
        in_specs=[smem_blk(lambda i, be, nu: (0, 0, 0)),
                  smem_blk(lambda i, be, nu: (jnp.minimum(i + 1, nb - 1), 0, 0)),
                  smem_blk(lambda i, be, nu: (jnp.where(i == 0, nb, i - 1), 0, 0)),
                  pl.BlockSpec(memory_space=pl.ANY),
                  wspec(D_MODEL, D_EXPERT), wspec(D_MODEL, D_EXPERT), wspec(D_EXPERT, D_MODEL)],
        out_specs=pl.BlockSpec(memory_space=pl.ANY),
        scratch_shapes=[pltpu.VMEM((2, bm, D_MODEL), F32), pltpu.VMEM((2, bm, D_MODEL), F32),
                        pltpu.VMEM((bm, D_MODEL), F32),
                        pltpu.VMEM((D_MODEL, 2 * D_EXPERT), BF16), pltpu.VMEM((D_EXPERT, D_MODEL), BF16),
                        pltpu.SemaphoreType.DMA((2,)), pltpu.SemaphoreType.DMA((2,)), pltpu.SemaphoreType.DMA(())],
    )
    return pl.pallas_call(
        functools.partial(_expert_kernel, bm=bm, nb=nb),
        grid_spec=grid_spec,
        out_shape=jax.ShapeDtypeStruct(((nb + 1) * bm, D_MODEL), F32),
        compiler_params=_cparams(("arbitrary",)),
        name="experts",
    )(bexp, n_used, slot_tok, slot_tok, dst_ext, x1, w_gate, w_up, w_down)


def _post_moe_kernel(x1_ref, y0_ref, y1_ref, route_ref, g_ref, b_ref, o_ref):
    r = route_ref[...]
    f = y0_ref[...] * r[:, 2:3] + y1_ref[...] * r[:, 3:4]
    o_ref[...] = _layer_norm(DN_ALPHA * x1_ref[...] + f, g_ref[...], b_ref[...])


def _post_moe(x1, y, route, g, b):
    n = x1.shape[0]
    tm = TOKEN_TILE
    nt = n // tm
    full = lambda arr: pl.BlockSpec(arr.shape, lambda i: (0,) * arr.ndim)
    return pl.pallas_call(
        _post_moe_kernel,
        grid=(nt,),
        in_specs=[pl.BlockSpec((tm, D_MODEL), lambda i: (i, 0)),
                  pl.BlockSpec((tm, D_MODEL), lambda i: (i, 0)),
                  pl.BlockSpec((tm, D_MODEL), lambda i: (nt + i, 0)),
                  pl.BlockSpec((tm, ROUTE_LANES), lambda i: (i, 0)),
                  full(g), full(b)],
        out_specs=pl.BlockSpec((tm, D_MODEL), lambda i: (i, 0)),
        out_shape=jax.ShapeDtypeStruct((n, D_MODEL), F32),
        compiler_params=_cparams(("parallel",)),
        name="post_moe",
    )(x1, y, y, route, g, b)


def _prep_even(w_in, w_uq, w_ukv, w_o):
    c0 = 3 * A_W + B_Q_RANK + B_KV_RANK
    win = jnp.zeros((D_MODEL, EVEN_IN_PAD), F32)
    win = win.at[:, :c0].set(w_in[:, :c0])
    kr0 = EVEN_IN_PAD - LANES + KR_LANE0
    win = win.at[:, kr0:kr0 + B_ROPE].set(w_in[:, c0:c0 + B_ROPE])
    wuq = w_uq.reshape(B_Q_RANK, B_HEADS, B_NOPE + B_ROPE)
    wuq = jnp.pad(wuq, ((0, 0), (0, 0), (0, LANES - B_NOPE - B_ROPE))).reshape(B_Q_RANK, B_HEADS * LANES)
    wukv = w_ukv.reshape(B_KV_RANK, B_HEADS, B_NOPE + B_V)
    wk = jnp.pad(wukv[:, :, :B_NOPE], ((0, 0), (0, 0), (0, LANES - B_NOPE))).reshape(B_KV_RANK, B_HEADS * LANES)
    wv = wukv[:, :, B_NOPE:].reshape(B_KV_RANK, B_HEADS * B_V)
    return win.astype(BF16), wuq.astype(BF16), wk.astype(BF16), wv.astype(BF16), w_o.astype(BF16)


def _prep_router(w_rg, b_rg, w_re, b_re):
    wr = jnp.zeros((D_MODEL, ROUTE_LANES), F32)
    wr = wr.at[:, :N_GROUPS].set(w_rg).at[:, N_GROUPS:N_GROUPS + N_EXPERTS].set(w_re)
    br = jnp.zeros((1, ROUTE_LANES), F32)
    br = br.at[0, :N_GROUPS].set(b_rg).at[0, N_GROUPS:N_GROUPS + N_EXPERTS].set(b_re)
    wr_hi = wr.astype(BF16)
    wr_lo = (wr - wr_hi.astype(F32)).astype(BF16)
    return wr_hi, wr_lo, br


def kernel(x, even_w_in, even_lam_q1, even_lam_k1, even_lam_q2, even_lam_k2, even_subln_g, even_q_norm_g, even_w_uq, even_kv_norm_g, even_w_ukv, even_w_o, odd_w_qkv, odd_rel_bias, odd_w_o, ln1_g, ln1_b, ln2_g, ln2_b, w_router_group, b_router_group, w_router_expert, b_router_expert, w_gate, w_up, w_down):
    bsz, seq, d = x.shape
    n = bsz * seq
    rows = seq // GRID_W
    x2d = x.reshape(n, d)

    a_scale = A_HEAD_DIM ** -0.5 * LOG2E
    b_scale = (B_NOPE + B_ROPE) ** -0.5 * LOG2E
    a_lanes = (0, A_HEAD_DIM)
    rope_tabs = jnp.concatenate([
        _rope_block_tables(seq, A_ROT, a_lanes, a_scale),
        _rope_block_tables(seq, A_ROT, a_lanes, 1.0),
        _rope_block_tables(seq, B_ROPE, (B_NOPE,), b_scale),
        _rope_block_tables(seq, B_ROPE, (KR_LANE0,), 1.0)], axis=0)

    for i in range(DEPTH):
        j = i // 2
        if i % 2 == 0:
            lambda_init = 0.8 - 0.6 * math.exp(-0.3 * i)
            win, wuq, wk, wv, w_o = _prep_even(even_w_in[j], even_w_uq[j], even_w_ukv[j], even_w_o[j])
            lam = (jnp.exp(jnp.sum(even_lam_q1[j] * even_lam_k1[j]))
                   - jnp.exp(jnp.sum(even_lam_q2[j] * even_lam_k2[j])) + lambda_init).reshape(1).astype(F32)
            qkva, qb, kb, vb = _even_proj(x2d, win, wuq, wk, wv, even_q_norm_g[j].reshape(1, -1),
                                          even_kv_norm_g[j].reshape(1, -1), rope_tabs, seq)
            by_seq = lambda t: t.reshape(t.shape[0], bsz, seq, LANES)
            a_out = _diff_attn(by_seq(qkva), lam, even_subln_g[j].reshape(1, -1), 1.0 - lambda_init)
            b_out = _mla_attn(by_seq(qb), by_seq(kb), by_seq(vb))
            acts = [a_out.reshape(-1, n, LANES), b_out.reshape(-1, n, LANES)]
        else:
            qkv = _proj(x2d, odd_w_qkv[j].astype(BF16), C_WIDTH, C_HEAD_DIM ** -0.5 * LOG2E)
            o = _natten(qkv.reshape(-1, bsz, seq, LANES), _natten_tables(odd_rel_bias[j] * LOG2E, rows))
            acts = [o.reshape(-1, n, LANES)]
            w_o = odd_w_o[j].astype(BF16)
        wr_hi, wr_lo, br = _prep_router(w_router_group[i], b_router_group[i], w_router_expert[i], b_router_expert[i])
        x1, route = _post_attn(acts, w_o, x2d,ln1_g[i].reshape(1, -1), ln1_b[i].reshape(1, -1), wr_hi, wr_lo, br)
        plan = _moe_plan(route, n, MOE_BLOCK)
        y = _experts(x1, plan, w_gate, w_up, w_down, i)
        x2d = _post_moe(x1, y, route,ln2_g[i].reshape(1, -1), ln2_b[i].reshape(1, -1))
    return x2d.reshape(bsz, seq, d)
```

```python
import functools
import math

import numpy as np
import jax
import jax.numpy as jnp
from jax import lax
from jax.experimental import pallas as pl
from jax.experimental.pallas import tpu as pltpu

F32 = jnp.float32
BF16 = jnp.bfloat16

D_MODEL = 1024
DEPTH = 4
GRID_W = 64
ROPE_THETA = 500000.0
A_HEADS = 4
A_HEAD_DIM = 64
A_ROT = A_HEAD_DIM // 4
A_W = A_HEADS * 2 * A_HEAD_DIM
B_HEADS = 8
B_NOPE = 64
B_ROPE = 32
B_V = 64
B_Q_RANK = 256
B_KV_RANK = 128
C_HEADS = 16
C_HEAD_DIM = 64
C_WIDTH = C_HEADS * C_HEAD_DIM
NA_ROWS = 8
NA_COLS = 16
N_GROUPS = 4
EXPERTS_PER_GROUP = 8
N_EXPERTS = N_GROUPS * EXPERTS_PER_GROUP
TOP_K = 2
D_EXPERT = 512
DN_ALPHA = (2 * DEPTH) ** 0.25
LN_EPS = 1e-5
RMS_EPS = 1e-6

LANES = 128
VMEM_LIMIT_BYTES = 52 * 1024 * 1024

EVEN_IN_PAD = 2048
KR_LANE0 = 64
TOKEN_TILE = 512
ATTN_Q_TILE = 2048
ATTN_ROW_SPLIT = 512
ATTN_JOB_LAG = 1
NA_JOB_LAG = 6
NA_Q_ROWS = 4
NA_Q_TILE = NA_Q_ROWS * GRID_W
NA_BAND_CHUNKS = 3
NA_BATCH_TILE = 8
MOE_BLOCK = 256
NEG_BIG = -1e30
LOG2E = math.log2(math.e)
ROUTE_LANES = LANES


def _cparams(sem):
    return pltpu.CompilerParams(dimension_semantics=sem, vmem_limit_bytes=VMEM_LIMIT_BYTES)


def _rope_block_tables(seq, rot_dim, lane_starts, scale):
    half = rot_dim // 2
    inv_freq = ROPE_THETA ** (-jnp.arange(0, rot_dim, 2, dtype=F32) / rot_dim)
    ang = jnp.arange(seq, dtype=F32)[:, None] * inv_freq[None, :]
    cos, sin = jnp.cos(ang), jnp.sin(ang)
    c = jnp.ones((seq, LANES), F32)
    a = jnp.zeros((seq, LANES), F32)
    b = jnp.zeros((seq, LANES), F32)
    for s0 in lane_starts:
        c = c.at[:, s0:s0 + half].set(cos).at[:, s0 + half:s0 + 2 * half].set(cos)
        a = a.at[:, s0:s0 + half].set(-sin)
        b = b.at[:, s0 + half:s0 + 2 * half].set(sin)
    return jnp.stack([c, a, b]) * scale


def _rope128(x, c, a, b, half):
    return x * c + pltpu.roll(x, LANES - half, 1) * a + pltpu.roll(x, half, 1) * b


def _rms(x, g):
    return x * lax.rsqrt(jnp.mean(x * x, axis=-1, keepdims=True) + RMS_EPS) * g


def _even_proj_kernel(x_ref, win_ref, wuq_ref, wk_ref, wv_ref, qg_ref, kvg_ref, t_ref,
                      qkva_ref, qb_ref, kb_ref, vb_ref):
    xb = x_ref[...].astype(BF16)
    h = jnp.dot(xb, win_ref[...], preferred_element_type=F32)
    ha = A_ROT // 2
    hb = B_ROPE // 2
    for hd in range(A_HEADS):
        lo = hd * LANES
        qkva_ref[hd] = _rope128(h[:, lo:lo + LANES], t_ref[0], t_ref[1], t_ref[2], ha).astype(BF16)
        lo = A_W + hd * LANES
        qkva_ref[A_HEADS + hd] = _rope128(h[:, lo:lo + LANES], t_ref[3], t_ref[4], t_ref[5], ha).astype(BF16)
        lo = 2 * A_W + hd * LANES
        qkva_ref[2 * A_HEADS + hd] = h[:, lo:lo + LANES].astype(BF16)
    c0 = 3 * A_W
    cq = _rms(h[:, c0:c0 + B_Q_RANK], qg_ref[...])
    qb = jnp.dot(cq.astype(BF16), wuq_ref[...], preferred_element_type=F32)
    ckv = _rms(h[:, c0 + B_Q_RANK:c0 + B_Q_RANK + B_KV_RANK], kvg_ref[...]).astype(BF16)
    kn = jnp.dot(ckv, wk_ref[...], preferred_element_type=F32)
    vb = jnp.dot(ckv, wv_ref[...], preferred_element_type=F32).astype(BF16)
    for p in range(B_HEADS // 2):
        vb_ref[p] = vb[:, p * LANES:(p + 1) * LANES]
    kr = _rope128(h[:, EVEN_IN_PAD - LANES:], t_ref[9], t_ref[10], t_ref[11], hb)
    for hd in range(B_HEADS):
        lo = hd * LANES
        qb_ref[hd] = _rope128(qb[:, lo:lo + LANES], t_ref[6], t_ref[7], t_ref[8], hb).astype(BF16)
        kb_ref[hd] = (kn[:, lo:lo + LANES] + kr).astype(BF16)


def _even_proj(x2d, win, wuq, wk, wv, qg, kvg, tabs, seq):
    n = x2d.shape[0]
    tm = TOKEN_TILE
    assert n % tm == 0 and seq % tm == 0
    spb = seq // tm
    full = lambda shape: pl.BlockSpec(shape, lambda i: (0,) * len(shape))
    row = lambda w: pl.BlockSpec((tm, w), lambda i: (i, 0))
    chunks = lambda c: pl.BlockSpec((c, tm, LANES), lambda i: (0, i, 0))
    return pl.pallas_call(
        _even_proj_kernel,
        grid=(n // tm,),
        in_specs=[row(D_MODEL), full(win.shape), full(wuq.shape), full(wk.shape), full(wv.shape),
                  full(qg.shape), full(kvg.shape),
                  pl.BlockSpec((12, tm, LANES), lambda i: (0, i % spb, 0))],
        out_specs=[chunks(3 * A_HEADS), chunks(B_HEADS), chunks(B_HEADS), chunks(B_HEADS // 2)],
        out_shape=[jax.ShapeDtypeStruct((3 * A_HEADS, n, LANES), BF16),
                   jax.ShapeDtypeStruct((B_HEADS, n, LANES), BF16),
                   jax.ShapeDtypeStruct((B_HEADS, n, LANES), BF16),
                   jax.ShapeDtypeStruct((B_HEADS // 2, n, LANES), BF16)],
        compiler_params=_cparams(("parallel",)),
        name="even_proj",
    )(x2d, win, wuq, wk, wv, qg, kvg, tabs)


_NT = (((1,), (1,)), ((), ()))


def _zero_after(m, dtype):
    u = pltpu.bitcast(jnp.broadcast_to(m, (m.shape[0], LANES)), jnp.uint32)
    return pltpu.bitcast((u >> 16) >> 16, F32).astype(dtype)


def _softmax_pv_jobs(jobs, lag):
    staged = []
    for n, (q, k, v_ones, bias) in enumerate(jobs):
        if n >= lag:
            q = q + _zero_after(staged[n - lag][1], q.dtype)
        s = lax.dot_general(q, k, _NT, preferred_element_type=F32)
        if bias is not None:
            s = s + bias
        staged.append((s, jnp.max(s, axis=-1, keepdims=True), v_ones))
    outs = []
    for s, m, v_ones in staged:
        e = jnp.exp2(s - m).astype(BF16)
        ol = jnp.dot(e, v_ones, preferred_element_type=F32)
        outs.append(ol[:, :LANES] / ol[:, LANES:])
    return outs


def _fill_v_ones(vext_ref, v):
    vext_ref[:, :LANES] = v
    vext_ref[:, LANES:] = jnp.ones(v.shape, v.dtype)


def _diff_attn_kernel(lam_ref, q_ref, k_ref, v_ref, g_ref, o_ref, vext_ref, *, out_scale):
    @pl.when(pl.program_id(2) == 0)
    def _():
        _fill_v_ones(vext_ref, v_ref[0])

    k = k_ref[0]
    v_ones = vext_ref[...]
    tq = q_ref.shape[1]
    lane = lax.broadcasted_iota(jnp.int32, (ATTN_ROW_SPLIT, LANES), 1)
    jobs = []
    for r0 in range(0, tq, ATTN_ROW_SPLIT):
        q = q_ref[0, r0:r0 + ATTN_ROW_SPLIT]
        zero = jnp.zeros_like(q)
        jobs.append((jnp.where(lane < A_HEAD_DIM, q, zero), k, v_ones, None))
        jobs.append((jnp.where(lane >= A_HEAD_DIM, q, zero), k, v_ones, None))
    outs = _softmax_pv_jobs(jobs, ATTN_JOB_LAG)
    for n, r0 in enumerate(range(0, tq, ATTN_ROW_SPLIT)):
        o = outs[2 * n] - lam_ref[0] * outs[2 * n + 1]
        o_ref[0, r0:r0 + ATTN_ROW_SPLIT] = (_rms(o, g_ref[...]) * out_scale).astype(BF16)


def _diff_attn(qkva, lam, subln_g, out_scale):
    _, b, s, _ = qkva.shape
    tq = ATTN_Q_TILE
    blk = lambda rows: (None, 1, rows, LANES)
    return pl.pallas_call(
        functools.partial(_diff_attn_kernel, out_scale=out_scale),
        grid=(b, A_HEADS, s // tq),
        in_specs=[pl.BlockSpec(memory_space=pltpu.SMEM),
                  pl.BlockSpec(blk(tq), lambda bi, h, qi: (h, bi, qi, 0)),
                  pl.BlockSpec(blk(s), lambda bi, h, qi: (A_HEADS + h, bi, 0, 0)),
                  pl.BlockSpec(blk(s), lambda bi, h, qi: (2 * A_HEADS + h, bi, 0, 0)),
                  pl.BlockSpec((1, LANES), lambda bi, h, qi: (0, 0))],
        out_specs=pl.BlockSpec(blk(tq), lambda bi, h, qi: (h, bi, qi, 0)),
        out_shape=jax.ShapeDtypeStruct((A_HEADS, b, s, LANES), BF16),
        scratch_shapes=[pltpu.VMEM((s, 2 * LANES), BF16)],
        compiler_params=_cparams(("parallel", "parallel", "arbitrary")),
        name="diff_attn",
    )(lam, qkva, qkva, qkva, subln_g)


def _mla_attn_kernel(q_ref, k_ref, v_ref, o_ref, vext_ref):
    @pl.when(pl.program_id(2) == 0)
    def _():
        _fill_v_ones(vext_ref, v_ref[0])

    v_ones = vext_ref[...]
    tq = q_ref.shape[2]
    jobs = []
    for r0 in range(0, tq, ATTN_ROW_SPLIT):
        for j in range(2):
            jobs.append((q_ref[j, 0, r0:r0 + ATTN_ROW_SPLIT], k_ref[j, 0], v_ones, None))
    outs = _softmax_pv_jobs(jobs, ATTN_JOB_LAG)
    lane = lax.broadcasted_iota(jnp.int32, outs[0].shape, 1)
    for n, r0 in enumerate(range(0, tq, ATTN_ROW_SPLIT)):
        o_ref[0, r0:r0 + ATTN_ROW_SPLIT] = jnp.where(lane < B_V, outs[2 * n], outs[2 * n + 1]).astype(BF16)


def _mla_attn(qb, kb, vb):
    _, b, s, _ = qb.shape
    tq = ATTN_Q_TILE
    return pl.pallas_call(
        _mla_attn_kernel,
        grid=(b, B_HEADS // 2, s // tq),
        in_specs=[pl.BlockSpec((2, 1, tq, LANES), lambda bi, p, qi: (p, bi, qi, 0)),
                  pl.BlockSpec((2, 1, s, LANES), lambda bi, p, qi: (p, bi, 0, 0)),
                  pl.BlockSpec((None, 1, s, LANES), lambda bi, p, qi: (p, bi, 0, 0))],
        out_specs=pl.BlockSpec((None, 1, tq, LANES), lambda bi, p, qi: (p, bi, qi, 0)),
        out_shape=jax.ShapeDtypeStruct((B_HEADS // 2, b, s, LANES), BF16),
        scratch_shapes=[pltpu.VMEM((s, 2 * LANES), BF16)],
        compiler_params=_cparams(("parallel", "parallel", "arbitrary")),
        name="mla_attn",
    )(qb, kb, vb)


def _proj_kernel(x_ref, w_ref, o_ref, *, q_cols, q_scale):
    acc = jnp.dot(x_ref[...].astype(BF16), w_ref[...], preferred_element_type=F32)
    for c in range(o_ref.shape[0]):
        piece = acc[:, c * LANES:(c + 1) * LANES]
        o_ref[c] = (piece * q_scale if c * LANES < q_cols else piece).astype(o_ref.dtype)


def _proj(x2d, w, q_cols, q_scale):
    n, kdim = x2d.shape
    m = w.shape[1]
    tm = TOKEN_TILE
    return pl.pallas_call(
        functools.partial(_proj_kernel, q_cols=q_cols, q_scale=q_scale),
        grid=(n // tm,),
        in_specs=[pl.BlockSpec((tm, kdim), lambda i: (i, 0)), pl.BlockSpec((kdim, m), lambda i: (0, 0))],
        out_specs=pl.BlockSpec((m // LANES, tm, LANES), lambda i: (0, i, 0)),
        out_shape=jax.ShapeDtypeStruct((m // LANES, n, LANES), BF16),
        compiler_params=_cparams(("parallel",)),
        name="qkv_proj",
    )(x2d, w)


def _na_variant_blocks(rows):
    return (0, 1, rows // NA_Q_ROWS - 1)


def _na_band_start(i, rows):
    return jnp.clip(i - 1, 0, rows // NA_Q_ROWS - NA_BAND_CHUNKS)


def _natten_tables(rel_bias, rows):
    band_rows = NA_BAND_CHUNKS * NA_Q_ROWS
    heads = rel_bias.shape[0]
    qc = np.arange(GRID_W)
    c0 = np.clip(qc - NA_COLS // 2, 0, GRID_W - NA_COLS)
    col_in = (qc[None, :] >= c0[:, None]) & (qc[None, :] < c0[:, None] + NA_COLS)
    col_off = qc[None, :] - qc[:, None] + NA_COLS - 1
    csel = (col_in[:, :, None] & (col_off[:, :, None] == np.arange(2 * NA_COLS - 1))).astype(np.float32)
    tabs = []
    for i_rep in _na_variant_blocks(rows):
        bs = int(np.clip(i_rep - 1, 0, rows // NA_Q_ROWS - NA_BAND_CHUNKS)) * NA_Q_ROWS
        qr = NA_Q_ROWS * i_rep + np.arange(NA_Q_ROWS)
        kr = bs + np.arange(band_rows)
        r0 = np.clip(qr - NA_ROWS // 2, 0, rows - NA_ROWS)
        row_in = (kr[None, :] >= r0[:, None]) & (kr[None, :] < r0[:, None] + NA_ROWS)
        row_off = kr[None, :] - qr[:, None] + NA_ROWS - 1
        rsel = (row_in[:, :, None] & (row_off[:, :, None] == np.arange(2 * NA_ROWS - 1))).astype(np.float32)
        inwin = row_in[:, None, :, None] & col_in[None, :, None, :]
        u = jnp.einsum('hab,rsa->hrsb', rel_bias.astype(F32), rsel, precision=lax.Precision.HIGHEST)
        t = jnp.einsum('hrsb,cdb->hrcsd', u, csel, precision=lax.Precision.HIGHEST)
        t = jnp.where(inwin[None], t, NEG_BIG)
        tabs.append(t.reshape(heads, NA_Q_TILE, band_rows * GRID_W))
    return jnp.stack(tabs, axis=1)


def _natten_kernel(q_ref, k0_ref, k1_ref, k2_ref, v0_ref, v1_ref, v2_ref, t_ref, o_ref, *, bt):
    lane = lax.broadcasted_iota(jnp.int32, (NA_Q_TILE, LANES), 1)
    lo = lane < C_HEAD_DIM
    jobs = []
    for b in range(bt):
        q = q_ref[b]
        zero = jnp.zeros_like(q)
        k = jnp.concatenate([k0_ref[b], k1_ref[b], k2_ref[b]], axis=0)
        v = jnp.concatenate([v0_ref[b], v1_ref[b], v2_ref[b]], axis=0)
        v_ones = jnp.concatenate([v, jnp.ones(v.shape, v.dtype)], axis=1)
        jobs.append((jnp.where(lo, q, zero), k, v_ones, t_ref[0]))
        jobs.append((jnp.where(lo, zero, q), k, v_ones, t_ref[1]))
    outs = _softmax_pv_jobs(jobs, NA_JOB_LAG)
    for b in range(bt):
        o_ref[b] = jnp.where(lo, outs[2 * b], outs[2 * b + 1]).astype(BF16)


def _natten(qkv, tabs):
    _, b, s, _ = qkv.shape
    rows = s // GRID_W
    nblk = rows // NA_Q_ROWS
    bt = NA_BATCH_TILE if b % NA_BATCH_TILE == 0 else 1
    npair = C_HEADS // 2
    blk = (None, bt, NA_Q_TILE, LANES)

    def band_spec(col0, j):
        return pl.BlockSpec(blk, lambda p, i, bi: (col0 + p, bi, _na_band_start(i, rows) + j, 0))

    def variant(i):
        return jnp.where(i == 0, 0, jnp.where(i == nblk - 1, 2, 1))

    return pl.pallas_call(
        functools.partial(_natten_kernel, bt=bt),
        grid=(npair, nblk, b // bt),
        in_specs=[pl.BlockSpec(blk, lambda p, i, bi: (p, bi, i, 0))]
                 + [band_spec(npair, j) for j in range(NA_BAND_CHUNKS)]
                 + [band_spec(2 * npair, j) for j in range(NA_BAND_CHUNKS)]
                 + [pl.BlockSpec((2, None, NA_Q_TILE, NA_BAND_CHUNKS * NA_Q_TILE),
                                 lambda p, i, bi: (p, variant(i), 0, 0))],
        out_specs=pl.BlockSpec(blk, lambda p, i, bi: (p, bi, i, 0)),
        out_shape=jax.ShapeDtypeStruct((npair, b, s, LANES), BF16),
        compiler_params=_cparams(("parallel", "parallel", "parallel")),
        name="natten",
    )(qkv, qkv, qkv, qkv, qkv, qkv, qkv, tabs)


def _layer_norm(z, g, b):
    mu = jnp.mean(z, axis=-1, keepdims=True)
    zc = z - mu
    var = jnp.mean(zc * zc, axis=-1, keepdims=True)
    return zc * lax.rsqrt(var + LN_EPS) * g + b


def _route(logits):
    lane_i = lax.broadcasted_iota(jnp.int32, logits.shape, 1)
    lane = lane_i.astype(F32)
    big = float(LANES)
    gl = jnp.where(lane_i < N_GROUPS, logits, NEG_BIG)
    gmax = jnp.max(gl, axis=-1, keepdims=True)
    grp = jnp.min(jnp.where(gl == gmax, lane, big), axis=-1, keepdims=True)
    g_w = 1.0 / jnp.sum(jnp.exp(gl - gmax), axis=-1, keepdims=True)
    e_lo = grp * EXPERTS_PER_GROUP + N_GROUPS
    in_grp = jnp.logical_and(lane >= e_lo, lane < e_lo + EXPERTS_PER_GROUP)
    el = jnp.where(in_grp, logits, NEG_BIG)
    v1 = jnp.max(el, axis=-1, keepdims=True)
    i1 = jnp.min(jnp.where(el == v1, lane, big), axis=-1, keepdims=True)
    el2 = jnp.where(lane == i1, NEG_BIG, el)
    v2 = jnp.max(el2, axis=-1, keepdims=True)
    i2 = jnp.min(jnp.where(el2 == v2, lane, big), axis=-1, keepdims=True)
    ex = jnp.exp(v2 - v1)
    den = 1.0 + ex
    gate1 = (1.0 / den) * g_w
    gate2 = (ex / den) * g_w
    rec = jnp.where(lane_i == 0, i1 - N_GROUPS,
                    jnp.where(lane_i == 1, i2 - N_GROUPS,
                              jnp.where(lane_i == 2, gate1, jnp.where(lane_i == 3, gate2, 0.0))))
    return rec


def _post_attn_kernel(*refs, n_act):
    acts = refs[:n_act]
    w_ref, x_ref, g_ref, b_ref, wrh_ref, wrl_ref, br_ref, x1_ref, route_ref = refs[n_act:]
    a = jnp.concatenate([a_ref[c] for a_ref in acts for c in range(a_ref.shape[0])], axis=1)
    m = jnp.dot(a, w_ref[...], preferred_element_type=F32)
    y = _layer_norm(DN_ALPHA * x_ref[...] + m, g_ref[...], b_ref[...])
    x1_ref[...] = y
    y_hi = y.astype(BF16)
    y_lo = (y - y_hi.astype(F32)).astype(BF16)
    wrh = wrh_ref[...]
    logits = (jnp.dot(y_hi, wrh, preferred_element_type=F32)
              + jnp.dot(y_lo, wrh, preferred_element_type=F32)
              + jnp.dot(y_hi, wrl_ref[...], preferred_element_type=F32)) + br_ref[...]
    route_ref[...] = _route(logits)


def _post_attn(acts, w_o, x2d, g, b, wrh, wrl, br):
    n = x2d.shape[0]
    tm = TOKEN_TILE
    full = lambda arr: pl.BlockSpec(arr.shape, lambda i: (0,) * arr.ndim)
    row = lambda w: pl.BlockSpec((tm, w), lambda i: (i, 0))
    return pl.pallas_call(
        functools.partial(_post_attn_kernel, n_act=len(acts)),
        grid=(n // tm,),
        in_specs=[pl.BlockSpec((a.shape[0], tm, LANES), lambda i: (0, i, 0)) for a in acts]
                 + [full(w_o), row(D_MODEL), full(g), full(b), full(wrh), full(wrl), full(br)],
        out_specs=[row(D_MODEL), row(ROUTE_LANES)],
        out_shape=[jax.ShapeDtypeStruct((n, D_MODEL), F32), jax.ShapeDtypeStruct((n, ROUTE_LANES), F32)],
        compiler_params=_cparams(("parallel",)),
        name="post_attn",
    )(*acts, w_o, x2d, g, b, wrh, wrl, br)


def _moe_plan(route, n_tok, bm):
    n_assign = n_tok * TOP_K
    nb = n_assign // bm + N_EXPERTS
    e_flat = route[:, :TOP_K].astype(jnp.int32).reshape(n_assign)
    order = jnp.argsort(e_flat, stable=True).astype(jnp.int32)
    counts = jnp.sum((e_flat[:, None] == jnp.arange(N_EXPERTS, dtype=jnp.int32)[None, :]).astype(jnp.int32), axis=0)
    start = jnp.cumsum(counts) - counts
    padded = (counts + bm - 1) // bm * bm
    pad_end = jnp.cumsum(padded)
    pad_start = pad_end - padded
    n_used = (pad_end[-1] // bm).astype(jnp.int32)
    blk = jnp.arange(nb, dtype=jnp.int32)
    bexp = jnp.sum((pad_end[None, :] <= (blk * bm)[:, None]).astype(jnp.int32), axis=1)
    bexp = jnp.minimum(bexp, N_EXPERTS - 1)
    used = blk < n_used
    bexp = jnp.where(used, bexp, bexp[jnp.maximum(n_used - 1, 0)])
    j = (blk * bm - pad_start[bexp])[:, None] + jnp.arange(bm, dtype=jnp.int32)[None, :]
    cnt = counts[bexp][:, None]
    valid = (j < cnt) & used[:, None]
    a = order[jnp.clip(start[bexp][:, None] + j, 0, n_assign - 1)]
    slot_tok = jnp.where(valid, a // TOP_K, 0)
    pads_before = (pad_start - start)[bexp][:, None]
    pad_dst = jnp.where(used[:, None], n_assign + pads_before + (j - cnt),
                        (blk * bm)[:, None] + jnp.arange(bm, dtype=jnp.int32)[None, :])
    slot_dst = jnp.where(valid, (a % TOP_K) * n_tok + a // TOP_K, pad_dst)
    return (bexp.astype(jnp.int32), n_used.reshape(1), slot_tok.astype(jnp.int32).reshape(nb, 1, bm),
            slot_dst.astype(jnp.int32).reshape(nb, 1, bm))


def _expert_kernel(bexp_ref, nused_ref, tok0_ref, tokn_ref, dstp_ref, x_hbm, w1_ref, w3_ref, w2_ref,
                   out_hbm, xbuf, ybuf, zbuf, w13, w2b, gsem, ssem, zsem, *, bm, nb):
    i = pl.program_id(0)
    n_used = nused_ref[0]
    slot = i % 2
    half = bm // 2

    def gather_rows(tok_ref, s, r0, r1):
        for r in range(r0, r1):
            pltpu.make_async_copy(x_hbm.at[pl.ds(tok_ref[0, 0, r], 1)], xbuf.at[s, pl.ds(r, 1)],
                                  gsem.at[s]).start(priority=r % 2)

    def scatter_rows(s, r0, r1):
        for r in range(r0, r1):
            pltpu.make_async_copy(ybuf.at[s, pl.ds(r, 1)], out_hbm.at[pl.ds(dstp_ref[0, 0, r], 1)],
                                  ssem.at[s]).start(priority=r % 2)

    def wait_gather(s):
        pltpu.make_async_copy(xbuf.at[s], xbuf.at[s], gsem.at[s]).wait()

    def wait_scatter(s):
        pltpu.make_async_copy(ybuf.at[s], ybuf.at[s], ssem.at[s]).wait()

    @pl.when(i == 0)
    def _():
        zbuf[...] = jnp.zeros_like(zbuf)
        ybuf[1] = jnp.zeros((bm, D_MODEL), F32)
        gather_rows(tok0_ref, 0, 0, bm)

    @pl.when(i < n_used)
    def _():
        changed = jnp.logical_or(i == 0, bexp_ref[i] != bexp_ref[jnp.maximum(i - 1, 0)])

        @pl.when(changed)
        def _():
            w13[:, :D_EXPERT] = w1_ref[...].astype(BF16)
            w13[:, D_EXPERT:] = w3_ref[...].astype(BF16)
            w2b[...] = w2_ref[...].astype(BF16)

    def block_step(s):
        wait_gather(s)
        xb = xbuf[s].astype(BF16)
        gate = jnp.dot(xb, w13[:, :D_EXPERT], preferred_element_type=F32)
        gather_rows(tokn_ref, 1 - s, 0, half)
        up = jnp.dot(xb, w13[:, D_EXPERT:], preferred_element_type=F32)
        gather_rows(tokn_ref, 1 - s, half, bm)
        act = (gate * (1.0 / (1.0 + jnp.exp(-gate))) * up).astype(BF16)
        y_lo = jnp.dot(act, w2b[:, :D_MODEL // 2], preferred_element_type=F32)
        scatter_rows(1 - s, 0, half)
        y_hi = jnp.dot(act, w2b[:, D_MODEL // 2:], preferred_element_type=F32)
        scatter_rows(1 - s, half, bm)

        @pl.when(i >= 1)
        def _():
            wait_scatter(s)

        ybuf[s, :, :D_MODEL // 2] = y_lo
        ybuf[s, :, D_MODEL // 2:] = y_hi

    for s in range(2):
        pl.when(jnp.logical_and(i < n_used, slot == s))(functools.partial(block_step, s))

    @pl.when(i == n_used)
    def _():
        wait_gather(slot)
        scatter_rows(1 - slot, 0, bm)
        wait_scatter(0)
        wait_scatter(1)

    @pl.when(jnp.logical_and(i >= n_used, i < nb))
    def _():
        cp = pltpu.make_async_copy(zbuf, out_hbm.at[pl.ds(pl.multiple_of(i * bm, bm), bm)], zsem)
        cp.start()
        cp.wait()


def _experts(x1, plan, w_gate, w_up, w_down, layer):
    bexp, n_used, slot_tok, slot_dst = plan
    nb, _, bm = slot_tok.shape
    spare = (nb * bm + jnp.arange(bm, dtype=jnp.int32)).reshape(1, 1, bm)
    dst_ext = jnp.concatenate([slot_dst, spare], axis=0)
    smem_blk = lambda fn: pl.BlockSpec((1, 1, bm), fn, memory_space=pltpu.SMEM)
    wspec = lambda k, m: pl.BlockSpec((None, None, k, m),
                                      lambda i, be, nu: (layer, be[jnp.minimum(i, nb - 1)], 0, 0))
    grid_spec = pltpu.PrefetchScalarGridSpec(
        num_scalar_prefetch=2,
        grid=(nb + 1,),
        in_specs=[smem_blk(lambda i, be, nu: (0, 0, 0)),
                  smem_blk(lambda i, be, nu: (jnp.minimum(i + 1, nb - 1), 0, 0)),
                  smem_blk(lambda i, be, nu: (jnp.where(i == 0, nb, i - 1), 0, 0)),
                  pl.BlockSpec(memory_space=pl.ANY),
                  wspec(D_MODEL, D_EXPERT), wspec(D_MODEL, D_EXPERT), wspec(D_EXPERT, D_MODEL)],
        out_specs=pl.BlockSpec(memory_space=pl.ANY),
        scratch_shapes=[pltpu.VMEM((2, bm, D_MODEL), F32), pltpu.VMEM((2, bm, D_MODEL), F32),
                        pltpu.VMEM((bm, D_MODEL), F32),
                        pltpu.VMEM((D_MODEL, 2 * D_EXPERT), BF16), pltpu.VMEM((D_EXPERT, D_MODEL), BF16),
                        pltpu.SemaphoreType.DMA((2,)), pltpu.SemaphoreType.DMA((2,)), pltpu.SemaphoreType.DMA(())],
    )
    return pl.pallas_call(
        functools.partial(_expert_kernel, bm=bm, nb=nb),
        grid_spec=grid_spec,
        out_shape=jax.ShapeDtypeStruct(((nb + 1) * bm, D_MODEL), F32),
        compiler_params=_cparams(("arbitrary",)),
        name="experts",
    )(bexp, n_used, slot_tok, slot_tok, dst_ext, x1, w_gate, w_up, w_down)


def _post_moe_kernel(x1_ref, y0_ref, y1_ref, route_ref, g_ref, b_ref, o_ref):
    r = route_ref[...]
    f = y0_ref[...] * r[:, 2:3] + y1_ref[...] * r[:, 3:4]
    o_ref[...] = _layer_norm(DN_ALPHA * x1_ref[...] + f, g_ref[...], b_ref[...])


def _post_moe(x1, y, route, g, b):
    n = x1.shape[0]
    tm = TOKEN_TILE
    nt = n // tm
    full = lambda arr: pl.BlockSpec(arr.shape, lambda i: (0,) * arr.ndim)
    return pl.pallas_call(
        _post_moe_kernel,
        grid=(nt,),
        in_specs=[pl.BlockSpec((tm, D_MODEL), lambda i: (i, 0)),
                  pl.BlockSpec((tm, D_MODEL), lambda i: (i, 0)),
                  pl.BlockSpec((tm, D_MODEL), lambda i: (nt + i, 0)),
                  pl.BlockSpec((tm, ROUTE_LANES), lambda i: (i, 0)),
                  full(g), full(b)],
        out_specs=pl.BlockSpec((tm, D_MODEL), lambda i: (i, 0)),
        out_shape=jax.ShapeDtypeStruct((n, D_MODEL), F32),
        compiler_params=_cparams(("parallel",)),
        name="post_moe",
    )(x1, y, y, route, g, b)


def _prep_even(w_in, w_uq, w_ukv, w_o):
    c0 = 3 * A_W + B_Q_RANK + B_KV_RANK
    win = jnp.zeros((D_MODEL, EVEN_IN_PAD), F32)
    win = win.at[:, :c0].set(w_in[:, :c0])
    kr0 = EVEN_IN_PAD - LANES + KR_LANE0
    win = win.at[:, kr0:kr0 + B_ROPE].set(w_in[:, c0:c0 + B_ROPE])
    wuq = w_uq.reshape(B_Q_RANK, B_HEADS, B_NOPE + B_ROPE)
    wuq = jnp.pad(wuq, ((0, 0), (0, 0), (0, LANES - B_NOPE - B_ROPE))).reshape(B_Q_RANK, B_HEADS * LANES)
    wukv = w_ukv.reshape(B_KV_RANK, B_HEADS, B_NOPE + B_V)
    wk = jnp.pad(wukv[:, :, :B_NOPE], ((0, 0), (0, 0), (0, LANES - B_NOPE))).reshape(B_KV_RANK, B_HEADS * LANES)
    wv = wukv[:, :, B_NOPE:].reshape(B_KV_RANK, B_HEADS * B_V)
    return win.astype(BF16), wuq.astype(BF16), wk.astype(BF16), wv.astype(BF16), w_o.astype(BF16)


def _prep_router(w_rg, b_rg, w_re, b_re):
    wr = jnp.zeros((D_MODEL, ROUTE_LANES), F32)
    wr = wr.at[:, :N_GROUPS].set(w_rg).at[:, N_GROUPS:N_GROUPS + N_EXPERTS].set(w_re)
    br = jnp.zeros((1, ROUTE_LANES), F32)
    br = br.at[0, :N_GROUPS].set(b_rg).at[0, N_GROUPS:N_GROUPS + N_EXPERTS].set(b_re)
    wr_hi = wr.astype(BF16)
    wr_lo = (wr - wr_hi.astype(F32)).astype(BF16)
    return wr_hi, wr_lo, br


def kernel(x, even_w_in, even_lam_q1, even_lam_k1, even_lam_q2, even_lam_k2, even_subln_g, even_q_norm_g, even_w_uq, even_kv_norm_g, even_w_ukv, even_w_o, odd_w_qkv, odd_rel_bias, odd_w_o, ln1_g, ln1_b, ln2_g, ln2_b, w_router_group, b_router_group, w_router_expert, b_router_expert, w_gate, w_up, w_down):
    bsz, seq, d = x.shape
    n = bsz * seq
    rows = seq // GRID_W
    x2d = x.reshape(n, d)

    a_scale = A_HEAD_DIM ** -0.5 * LOG2E
    b_scale = (B_NOPE + B_ROPE) ** -0.5 * LOG2E
    a_lanes = (0, A_HEAD_DIM)
    rope_tabs = jnp.concatenate([
        _rope_block_tables(seq, A_ROT, a_lanes, a_scale),
        _rope_block_tables(seq, A_ROT, a_lanes, 1.0),
        _rope_block_tables(seq, B_ROPE, (B_NOPE,), b_scale),
        _rope_block_tables(seq, B_ROPE, (KR_LANE0,), 1.0)], axis=0)

    for i in range(DEPTH):
        j = i // 2
        if i % 2 == 0:
            lambda_init = 0.8 - 0.6 * math.exp(-0.3 * i)
            win, wuq, wk, wv, w_o = _prep_even(even_w_in[j], even_w_uq[j], even_w_ukv[j], even_w_o[j])
            lam = (jnp.exp(jnp.sum(even_lam_q1[j] * even_lam_k1[j]))
                   - jnp.exp(jnp.sum(even_lam_q2[j] * even_lam_k2[j])) + lambda_init).reshape(1).astype(F32)
            qkva, qb, kb, vb = _even_proj(x2d, win, wuq, wk, wv, even_q_norm_g[j].reshape(1, -1),
                                          even_kv_norm_g[j].reshape(1, -1), rope_tabs, seq)
            by_seq = lambda t: t.reshape(t.shape[0], bsz, seq, LANES)
            a_out = _diff_attn(by_seq(qkva), lam, even_subln_g[j].reshape(1, -1), 1.0 - lambda_init)
            b_out = _mla_attn(by_seq(qb), by_seq(kb), by_seq(vb))
            acts = [a_out.reshape(-1, n, LANES), b_out.reshape(-1, n, LANES)]
        else:
            qkv = _proj(x2d, odd_w_qkv[j].astype(BF16), C_WIDTH, C_HEAD_DIM ** -0.5 * LOG2E)
            o = _natten(qkv.reshape(-1, bsz, seq, LANES), _natten_tables(odd_rel_bias[j] * LOG2E, rows))
            acts = [o.reshape(-1, n, LANES)]
            w_o = odd_w_o[j].astype(BF16)
        wr_hi, wr_lo, br = _prep_router(w_router_group[i], b_router_group[i], w_router_expert[i], b_router_expert[i])
        x1, route = _post_attn(acts, w_o, x2d, ln1_g[i].reshape(1, -1), ln1_b[i].reshape(1, -1), wr_hi, wr_lo, br)
        plan = _moe_plan(route, n, MOE_BLOCK)
        y = _experts(x1, plan, w_gate, w_up, w_down, i)
        x2d = _post_moe(x1, y, route,ln2_g[i].reshape(1, -1), ln2_b[i].reshape(1, -1))
    return x2d.reshape(bsz, seq, d)
```

```python
import functools
import math

import numpy as np
import jax
import jax.numpy as jnp
from jax import lax
from jax.experimental import pallas as pl
from jax.experimental.pallas import tpu as pltpu

F32 = jnp.float32
BF16 = jnp.bfloat16

D_MODEL = 1024
DEPTH = 4
GRID_W = 64
ROPE_THETA = 500000.0
A_HEADS = 4
A_HEAD_DIM = 64
A_ROT = A_HEAD_DIM // 4
A_W = A_HEADS * 2 * A_HEAD_DIM
B_HEADS = 8
B_NOPE = 64
B_ROPE = 32
B_V = 64
B_Q_RANK = 256
B_KV_RANK = 128
C_HEADS = 16
C_HEAD_DIM = 64
C_WIDTH = C_HEADS * C_HEAD_DIM
NA_ROWS = 8
NA_COLS = 16
N_GROUPS = 4
EXPERTS_PER_GROUP = 8
N_EXPERTS = N_GROUPS * EXPERTS_PER_GROUP
TOP_K = 2
D_EXPERT = 512
DN_ALPHA = (2 * DEPTH) ** 0.25
LN_EPS = 1e-5
RMS_EPS = 1e-6

LANES = 128
ROW_TILES = D_MODEL // LANES
VMEM_LIMIT_BYTES = 52 * 1024 * 1024

EVEN_IN_PAD = 2048
KR_LANE0 = 64
TOKEN_TILE = 512
ATTN_Q_TILE = 2048
ATTN_ROW_SPLIT = 512
ATTN_JOB_LAG = 1
NA_JOB_LAG = 6
NA_Q_ROWS = 4
NA_Q_TILE = NA_Q_ROWS * GRID_W
NA_BAND_CHUNKS = 3
NA_BATCH_TILE = 8
MOE_BLOCK = 256
NEG_BIG = -1e30
LOG2E = math.log2(math.e)
ROUTE_LANES = LANES


def _cparams(sem):
    return pltpu.CompilerParams(dimension_semantics=sem, vmem_limit_bytes=VMEM_LIMIT_BYTES)


def _rope_block_tables(seq, rot_dim, lane_starts, scale):
    half = rot_dim // 2
    inv_freq = ROPE_THETA ** (-jnp.arange(0, rot_dim, 2, dtype=F32) / rot_dim)
    ang = jnp.arange(seq, dtype=F32)[:, None] * inv_freq[None, :]
    cos, sin = jnp.cos(ang), jnp.sin(ang)
    c = jnp.ones((seq, LANES), F32)
    a = jnp.zeros((seq, LANES), F32)
    b = jnp.zeros((seq, LANES), F32)
    for s0 in lane_starts:
        c = c.at[:, s0:s0 + half].set(cos).at[:, s0 + half:s0 + 2 * half].set(cos)
        a = a.at[:, s0:s0 + half].set(-sin)
        b = b.at[:, s0 + half:s0 + 2 * half].set(sin)
    return jnp.stack([c, a, b]) * scale


def _rope128(x, c, a, b, half):
    return x * c + pltpu.roll(x, LANES - half, 1) * a + pltpu.roll(x, half, 1) * b


def _rms(x, g):
    return x * lax.rsqrt(jnp.mean(x * x, axis=-1, keepdims=True) + RMS_EPS) * g


def _to_token_rows(y):
    pieces = [y[:, c * LANES:(c + 1) * LANES] for c in range(y.shape[1] // LANES)]
    return jnp.swapaxes(jnp.stack(pieces, axis=0), 0, 1)


def _from_token_rows(x3):
    xs = jnp.swapaxes(x3, 0, 1)
    return jnp.concatenate([xs[c] for c in range(xs.shape[0])], axis=1)


def _even_proj_kernel(x_ref, win_ref, wuq_ref, wk_ref, wv_ref, qg_ref, kvg_ref, t_ref,
                      qkva_ref, qb_ref, kb_ref, vb_ref):
    xb = x_ref[...].astype(BF16)
    h = jnp.dot(xb, win_ref[...], preferred_element_type=F32)
    ha = A_ROT // 2
    hb = B_ROPE // 2
    for hd in range(A_HEADS):
        lo = hd * LANES
        qkva_ref[hd] = _rope128(h[:, lo:lo + LANES], t_ref[0], t_ref[1], t_ref[2], ha).astype(BF16)
        lo = A_W + hd * LANES
        qkva_ref[A_HEADS + hd] = _rope128(h[:, lo:lo + LANES], t_ref[3], t_ref[4], t_ref[5], ha).astype(BF16)
        lo = 2 * A_W + hd * LANES
        qkva_ref[2 * A_HEADS + hd] = h[:, lo:lo + LANES].astype(BF16)
    c0 = 3 * A_W
    cq = _rms(h[:, c0:c0 + B_Q_RANK], qg_ref[...])
    qb = jnp.dot(cq.astype(BF16), wuq_ref[...], preferred_element_type=F32)
    ckv = _rms(h[:, c0 + B_Q_RANK:c0 + B_Q_RANK + B_KV_RANK], kvg_ref[...]).astype(BF16)
    kn = jnp.dot(ckv, wk_ref[...], preferred_element_type=F32)
    vb = jnp.dot(ckv, wv_ref[...], preferred_element_type=F32).astype(BF16)
    for p in range(B_HEADS // 2):
        vb_ref[p] = vb[:, p * LANES:(p + 1) * LANES]
    kr = _rope128(h[:, EVEN_IN_PAD - LANES:], t_ref[9], t_ref[10], t_ref[11], hb)
    for hd in range(B_HEADS):
        lo = hd * LANES
        qb_ref[hd] = _rope128(qb[:, lo:lo + LANES], t_ref[6], t_ref[7], t_ref[8], hb).astype(BF16)
        kb_ref[hd] = (kn[:, lo:lo + LANES] + kr).astype(BF16)


def _even_proj(x2d, win, wuq, wk, wv, qg, kvg, tabs, seq):
    n = x2d.shape[0]
    tm = TOKEN_TILE
    assert n % tm == 0 and seq % tm == 0
    spb = seq // tm
    full = lambda shape: pl.BlockSpec(shape, lambda i: (0,) * len(shape))
    row = lambda w: pl.BlockSpec((tm, w), lambda i: (i, 0))
    chunks = lambda c: pl.BlockSpec((c, tm, LANES), lambda i: (0, i, 0))
    return pl.pallas_call(
        _even_proj_kernel,
        grid=(n // tm,),
        in_specs=[row(D_MODEL), full(win.shape), full(wuq.shape), full(wk.shape), full(wv.shape),
                  full(qg.shape), full(kvg.shape),
                  pl.BlockSpec((12, tm, LANES), lambda i: (0, i % spb, 0))],
        out_specs=[chunks(3 * A_HEADS), chunks(B_HEADS), chunks(B_HEADS), chunks(B_HEADS // 2)],
        out_shape=[jax.ShapeDtypeStruct((3 * A_HEADS, n, LANES), BF16),
                   jax.ShapeDtypeStruct((B_HEADS, n, LANES), BF16),
                   jax.ShapeDtypeStruct((B_HEADS, n, LANES), BF16),
                   jax.ShapeDtypeStruct((B_HEADS // 2, n, LANES), BF16)],
        compiler_params=_cparams(("parallel",)),
        name="even_proj",
    )(x2d, win, wuq, wk, wv, qg, kvg, tabs)


_NT = (((1,), (1,)), ((), ()))


def _zero_after(m, dtype):
    u = pltpu.bitcast(jnp.broadcast_to(m, (m.shape[0], LANES)), jnp.uint32)
    return pltpu.bitcast((u >> 16) >> 16, F32).astype(dtype)


def _softmax_pv_jobs(jobs, lag):
    staged = []
    for n, (q, k, v_ones, bias) in enumerate(jobs):
        if n >= lag:
            q = q + _zero_after(staged[n - lag][1], q.dtype)
        s = lax.dot_general(q, k, _NT, preferred_element_type=F32)
        if bias is not None:
            s = s + bias
        staged.append((s, jnp.max(s, axis=-1, keepdims=True), v_ones))
    outs = []
    for s, m, v_ones in staged:
        e = jnp.exp2(s - m).astype(BF16)
        ol = jnp.dot(e, v_ones, preferred_element_type=F32)
        outs.append(ol[:, :LANES] / ol[:, LANES:])
    return outs


def _fill_v_ones(vext_ref, v):
    vext_ref[:, :LANES] = v
    vext_ref[:, LANES:] = jnp.ones(v.shape, v.dtype)


def _diff_attn_kernel(lam_ref, q_ref, k_ref, v_ref, g_ref, o_ref, vext_ref, *, out_scale):
    @pl.when(pl.program_id(2) == 0)
    def _():
        _fill_v_ones(vext_ref, v_ref[0])

    k = k_ref[0]
    v_ones = vext_ref[...]
    tq = q_ref.shape[1]
    lane = lax.broadcasted_iota(jnp.int32, (ATTN_ROW_SPLIT, LANES), 1)
    jobs = []
    for r0 in range(0, tq, ATTN_ROW_SPLIT):
        q = q_ref[0, r0:r0 + ATTN_ROW_SPLIT]
        zero = jnp.zeros_like(q)
        jobs.append((jnp.where(lane < A_HEAD_DIM, q, zero), k, v_ones, None))
        jobs.append((jnp.where(lane >= A_HEAD_DIM, q, zero), k, v_ones, None))
    outs = _softmax_pv_jobs(jobs, ATTN_JOB_LAG)
    for n, r0 in enumerate(range(0, tq, ATTN_ROW_SPLIT)):
        o = outs[2 * n] - lam_ref[0] * outs[2 * n + 1]
        o_ref[0, r0:r0 + ATTN_ROW_SPLIT] = (_rms(o, g_ref[...]) * out_scale).astype(BF16)


def _diff_attn(qkva, lam, subln_g, out_scale):
    _, b, s, _ = qkva.shape
    tq = ATTN_Q_TILE
    blk = lambda rows: (None, 1, rows, LANES)
    return pl.pallas_call(
        functools.partial(_diff_attn_kernel, out_scale=out_scale),
        grid=(b, A_HEADS, s // tq),
        in_specs=[pl.BlockSpec(memory_space=pltpu.SMEM),
                  pl.BlockSpec(blk(tq), lambda bi, h, qi: (h, bi, qi, 0)),
                  pl.BlockSpec(blk(s), lambda bi, h, qi: (A_HEADS + h, bi, 0, 0)),
                  pl.BlockSpec(blk(s), lambda bi, h, qi: (2 * A_HEADS + h, bi, 0, 0)),
                  pl.BlockSpec((1, LANES), lambda bi, h, qi: (0, 0))],
        out_specs=pl.BlockSpec(blk(tq), lambda bi, h, qi: (h, bi, qi, 0)),
        out_shape=jax.ShapeDtypeStruct((A_HEADS, b, s, LANES), BF16),
        scratch_shapes=[pltpu.VMEM((s, 2 * LANES), BF16)],
        compiler_params=_cparams(("parallel", "parallel", "arbitrary")),
        name="diff_attn",
    )(lam, qkva, qkva, qkva, subln_g)


def _mla_attn_kernel(q_ref, k_ref, v_ref, o_ref, vext_ref):
    @pl.when(pl.program_id(2) == 0)
    def _():
        _fill_v_ones(vext_ref, v_ref[0])

    v_ones = vext_ref[...]
    tq = q_ref.shape[2]
    jobs = []
    for r0 in range(0, tq, ATTN_ROW_SPLIT):
        for j in range(2):
            jobs.append((q_ref[j, 0, r0:r0 + ATTN_ROW_SPLIT], k_ref[j, 0], v_ones, None))
    outs = _softmax_pv_jobs(jobs, ATTN_JOB_LAG)
    lane = lax.broadcasted_iota(jnp.int32, outs[0].shape, 1)
    for n, r0 in enumerate(range(0, tq, ATTN_ROW_SPLIT)):
        o_ref[0, r0:r0 + ATTN_ROW_SPLIT] = jnp.where(lane < B_V, outs[2 * n], outs[2 * n + 1]).astype(BF16)


def _mla_attn(qb, kb, vb):
    _, b, s, _ = qb.shape
    tq = ATTN_Q_TILE
    return pl.pallas_call(
        _mla_attn_kernel,
        grid=(b, B_HEADS // 2, s // tq),
        in_specs=[pl.BlockSpec((2, 1, tq, LANES), lambda bi, p, qi: (p, bi, qi, 0)),
                  pl.BlockSpec((2, 1, s, LANES), lambda bi, p, qi: (p, bi, 0, 0)),
                  pl.BlockSpec((None, 1, s, LANES), lambda bi, p, qi: (p, bi, 0, 0))],
        out_specs=pl.BlockSpec((None, 1, tq, LANES), lambda bi, p, qi: (p, bi, qi, 0)),
        out_shape=jax.ShapeDtypeStruct((B_HEADS // 2, b, s, LANES), BF16),
        scratch_shapes=[pltpu.VMEM((s, 2 * LANES), BF16)],
        compiler_params=_cparams(("parallel", "parallel", "arbitrary")),
        name="mla_attn",
    )(qb, kb, vb)


def _proj_kernel(x_ref, w_ref, o_ref, *, q_cols, q_scale):
    acc = jnp.dot(x_ref[...].astype(BF16), w_ref[...], preferred_element_type=F32)
    for c in range(o_ref.shape[0]):
        piece = acc[:, c * LANES:(c + 1) * LANES]
        o_ref[c] = (piece * q_scale if c * LANES < q_cols else piece).astype(o_ref.dtype)


def _proj(x2d, w, q_cols, q_scale):
    n, kdim = x2d.shape
    m = w.shape[1]
    tm = TOKEN_TILE
    return pl.pallas_call(
        functools.partial(_proj_kernel, q_cols=q_cols, q_scale=q_scale),
        grid=(n // tm,),
        in_specs=[pl.BlockSpec((tm, kdim), lambda i: (i, 0)), pl.BlockSpec((kdim, m), lambda i: (0, 0))],
        out_specs=pl.BlockSpec((m // LANES, tm, LANES), lambda i: (0, i, 0)),
        out_shape=jax.ShapeDtypeStruct((m // LANES, n, LANES), BF16),
        compiler_params=_cparams(("parallel",)),
        name="qkv_proj",
    )(x2d, w)


def _na_variant_blocks(rows):
    return (0, 1, rows // NA_Q_ROWS - 1)


def _na_band_start(i, rows):
    return jnp.clip(i - 1, 0, rows // NA_Q_ROWS - NA_BAND_CHUNKS)


def _natten_tables(rel_bias, rows):
    band_rows = NA_BAND_CHUNKS * NA_Q_ROWS
    heads = rel_bias.shape[0]
    qc = np.arange(GRID_W)
    c0 = np.clip(qc - NA_COLS // 2, 0, GRID_W - NA_COLS)
    col_in = (qc[None, :] >= c0[:, None]) & (qc[None, :] < c0[:, None] + NA_COLS)
    col_off = qc[None, :] - qc[:, None] + NA_COLS - 1
    csel = (col_in[:, :, None] & (col_off[:, :, None] == np.arange(2 * NA_COLS - 1))).astype(np.float32)
    tabs = []
    for i_rep in _na_variant_blocks(rows):
        bs = int(np.clip(i_rep - 1, 0, rows // NA_Q_ROWS - NA_BAND_CHUNKS)) * NA_Q_ROWS
        qr = NA_Q_ROWS * i_rep + np.arange(NA_Q_ROWS)
        kr = bs + np.arange(band_rows)
        r0 = np.clip(qr - NA_ROWS // 2, 0, rows - NA_ROWS)
        row_in = (kr[None, :] >= r0[:, None]) & (kr[None, :] < r0[:, None] + NA_ROWS)
        row_off = kr[None, :] - qr[:, None] + NA_ROWS - 1
        rsel = (row_in[:, :, None] & (row_off[:, :, None] == np.arange(2 * NA_ROWS - 1))).astype(np.float32)
        inwin = row_in[:, None, :, None] & col_in[None, :, None, :]
        u = jnp.einsum('hab,rsa->hrsb', rel_bias.astype(F32), rsel, precision=lax.Precision.HIGHEST)
        t = jnp.einsum('hrsb,cdb->hrcsd', u, csel, precision=lax.Precision.HIGHEST)
        t = jnp.where(inwin[None], t, NEG_BIG)
        tabs.append(t.reshape(heads, NA_Q_TILE, band_rows * GRID_W))
    return jnp.stack(tabs, axis=1)


def _natten_kernel(q_ref, k0_ref, k1_ref, k2_ref, v0_ref, v1_ref, v2_ref, t_ref, o_ref, *, bt):
    lane = lax.broadcasted_iota(jnp.int32, (NA_Q_TILE, LANES), 1)
    lo = lane < C_HEAD_DIM
    jobs = []
    for b in range(bt):
        q = q_ref[b]
        zero = jnp.zeros_like(q)
        k = jnp.concatenate([k0_ref[b], k1_ref[b], k2_ref[b]], axis=0)
        v = jnp.concatenate([v0_ref[b], v1_ref[b], v2_ref[b]], axis=0)
        v_ones = jnp.concatenate([v, jnp.ones(v.shape, v.dtype)], axis=1)
        jobs.append((jnp.where(lo, q, zero), k, v_ones, t_ref[0]))
        jobs.append((jnp.where(lo, zero, q), k, v_ones, t_ref[1]))
    outs = _softmax_pv_jobs(jobs, NA_JOB_LAG)
    for b in range(bt):
        o_ref[b] = jnp.where(lo, outs[2 * b], outs[2 * b + 1]).astype(BF16)


def _natten(qkv, tabs):
    _, b, s, _ = qkv.shape
    rows = s // GRID_W
    nblk = rows // NA_Q_ROWS
    bt = NA_BATCH_TILE if b % NA_BATCH_TILE == 0 else 1
    npair = C_HEADS // 2
    blk = (None, bt, NA_Q_TILE, LANES)

    def band_spec(col0, j):
        return pl.BlockSpec(blk, lambda p, i, bi: (col0 + p, bi, _na_band_start(i, rows) + j, 0))

    def variant(i):
        return jnp.where(i == 0, 0, jnp.where(i == nblk - 1, 2, 1))

    return pl.pallas_call(
        functools.partial(_natten_kernel, bt=bt),
        grid=(npair, nblk, b // bt),
        in_specs=[pl.BlockSpec(blk, lambda p, i, bi: (p, bi, i, 0))]
                 + [band_spec(npair, j) for j in range(NA_BAND_CHUNKS)]
                 + [band_spec(2 * npair, j) for j in range(NA_BAND_CHUNKS)]
                 + [pl.BlockSpec((2, None, NA_Q_TILE, NA_BAND_CHUNKS * NA_Q_TILE),
                                 lambda p, i, bi: (p, variant(i), 0, 0))],
        out_specs=pl.BlockSpec(blk, lambda p, i, bi: (p, bi, i, 0)),
        out_shape=jax.ShapeDtypeStruct((npair, b, s, LANES), BF16),
        compiler_params=_cparams(("parallel", "parallel", "parallel")),
        name="natten",
    )(qkv, qkv, qkv, qkv, qkv, qkv, qkv, tabs)


def _layer_norm(z, g, b):
    mu = jnp.mean(z, axis=-1, keepdims=True)
    zc = z - mu
    var = jnp.mean(zc * zc, axis=-1, keepdims=True)
    return zc * lax.rsqrt(var + LN_EPS) * g + b


def _route(logits):
    lane_i = lax.broadcasted_iota(jnp.int32, logits.shape, 1)
    lane = lane_i.astype(F32)
    big = float(LANES)
    gl = jnp.where(lane_i < N_GROUPS, logits, NEG_BIG)
    gmax = jnp.max(gl, axis=-1, keepdims=True)
    grp = jnp.min(jnp.where(gl == gmax, lane, big), axis=-1, keepdims=True)
    g_w = 1.0 / jnp.sum(jnp.exp(gl - gmax), axis=-1, keepdims=True)
    e_lo = grp * EXPERTS_PER_GROUP + N_GROUPS
    in_grp = jnp.logical_and(lane >= e_lo, lane < e_lo + EXPERTS_PER_GROUP)
    el = jnp.where(in_grp, logits, NEG_BIG)
    v1 = jnp.max(el, axis=-1, keepdims=True)
    i1 = jnp.min(jnp.where(el == v1, lane, big), axis=-1, keepdims=True)
    el2 = jnp.where(lane == i1, NEG_BIG, el)
    v2 = jnp.max(el2, axis=-1, keepdims=True)
    i2 = jnp.min(jnp.where(el2 == v2, lane, big), axis=-1, keepdims=True)
    ex = jnp.exp(v2 - v1)
    den = 1.0 + ex
    gate1 = (1.0 / den) * g_w
    gate2 = (ex / den) * g_w
    rec = jnp.where(lane_i == 0, i1 - N_GROUPS,
                    jnp.where(lane_i == 1, i2 - N_GROUPS,
                              jnp.where(lane_i == 2, gate1, jnp.where(lane_i == 3, gate2, 0.0))))
    return rec


def _post_attn_kernel(*refs, n_act):
    acts = refs[:n_act]
    w_ref, x_ref, g_ref, b_ref, wrh_ref, wrl_ref, br_ref, x1_ref, route_ref = refs[n_act:]
    a = jnp.concatenate([a_ref[c] for a_ref in acts for c in range(a_ref.shape[0])], axis=1)
    m = jnp.dot(a, w_ref[...], preferred_element_type=F32)
    y = _layer_norm(DN_ALPHA * x_ref[...] + m, g_ref[...], b_ref[...])
    x1_ref[...] = _to_token_rows(y)
    y_hi = y.astype(BF16)
    y_lo = (y - y_hi.astype(F32)).astype(BF16)
    wrh = wrh_ref[...]
    logits = (jnp.dot(y_hi, wrh, preferred_element_type=F32)
              + jnp.dot(y_lo, wrh, preferred_element_type=F32)
              + jnp.dot(y_hi, wrl_ref[...], preferred_element_type=F32)) + br_ref[...]
    route_ref[...] = _route(logits)


def _post_attn(acts, w_o, x2d, g, b, wrh, wrl, br):
    n = x2d.shape[0]
    tm = TOKEN_TILE
    full = lambda arr: pl.BlockSpec(arr.shape, lambda i: (0,) * arr.ndim)
    row = lambda w: pl.BlockSpec((tm, w), lambda i: (i, 0))
    return pl.pallas_call(
        functools.partial(_post_attn_kernel, n_act=len(acts)),
        grid=(n // tm,),
        in_specs=[pl.BlockSpec((a.shape[0], tm, LANES), lambda i: (0, i, 0)) for a in acts]
                 + [full(w_o), row(D_MODEL), full(g), full(b), full(wrh), full(wrl), full(br)],
        out_specs=[pl.BlockSpec((tm, ROW_TILES, LANES), lambda i: (i, 0, 0)), row(ROUTE_LANES)],
        out_shape=[jax.ShapeDtypeStruct((n, ROW_TILES, LANES), F32), jax.ShapeDtypeStruct((n, ROUTE_LANES), F32)],
        compiler_params=_cparams(("parallel",)),
        name="post_attn",
    )(*acts, w_o, x2d, g, b, wrh, wrl, br)


def _moe_plan(route, n_tok, bm):
    n_assign = n_tok * TOP_K
    nb = n_assign // bm + N_EXPERTS
    e_flat = route[:, :TOP_K].astype(jnp.int32).reshape(n_assign)
    order = jnp.argsort(e_flat, stable=True).astype(jnp.int32)
    counts = jnp.sum((e_flat[:, None] == jnp.arange(N_EXPERTS, dtype=jnp.int32)[None, :]).astype(jnp.int32), axis=0)
    start = jnp.cumsum(counts) - counts
    padded = (counts + bm - 1) // bm * bm
    pad_end = jnp.cumsum(padded)
    pad_start = pad_end - padded
    n_used = (pad_end[-1] // bm).astype(jnp.int32)
    blk = jnp.arange(nb, dtype=jnp.int32)
    bexp = jnp.sum((pad_end[None, :] <= (blk * bm)[:, None]).astype(jnp.int32), axis=1)
    bexp = jnp.minimum(bexp, N_EXPERTS - 1)
    used = blk < n_used
    bexp = jnp.where(used, bexp, bexp[jnp.maximum(n_used - 1, 0)])
    j = (blk * bm - pad_start[bexp])[:, None] + jnp.arange(bm, dtype=jnp.int32)[None, :]
    cnt = counts[bexp][:, None]
    valid = (j < cnt) & used[:, None]
    a = order[jnp.clip(start[bexp][:, None] + j, 0, n_assign - 1)]
    slot_tok = jnp.where(valid, a // TOP_K, 0)
    pads_before = (pad_start - start)[bexp][:, None]
    pad_dst = jnp.where(used[:, None], n_assign + pads_before + (j - cnt),
                        (blk * bm)[:, None] + jnp.arange(bm, dtype=jnp.int32)[None, :])
    slot_dst = jnp.where(valid, (a % TOP_K) * n_tok + a // TOP_K, pad_dst)
    return (bexp.astype(jnp.int32), n_used.reshape(1), slot_tok.astype(jnp.int32).reshape(nb, 1, bm),
            slot_dst.astype(jnp.int32).reshape(nb, 1, bm))


def _expert_kernel(bexp_ref, nused_ref, tok0_ref, tokn_ref, dstp_ref, x_hbm, w1_ref, w3_ref, w2_ref,
                   out_hbm, xbuf, ybuf, zbuf, w13, w2b, gsem, ssem, zsem, *, bm, nb):
    i = pl.program_id(0)
    n_used = nused_ref[0]
    slot = i % 2
    half = bm // 2

    def gather_rows(tok_ref, s, r0, r1):
        for r in range(r0, r1):
            pltpu.make_async_copy(x_hbm.at[tok_ref[0, 0, r]], xbuf.at[s, r],
                                  gsem.at[s]).start(priority=r % 2)

    def scatter_rows(s, r0, r1):
        for r in range(r0, r1):
            pltpu.make_async_copy(ybuf.at[s, r], out_hbm.at[dstp_ref[0, 0, r]],
                                  ssem.at[s]).start(priority=r % 2)

    def wait_gather(s):
        pltpu.make_async_copy(xbuf.at[s], xbuf.at[s], gsem.at[s]).wait()

    def wait_scatter(s):
        pltpu.make_async_copy(ybuf.at[s], ybuf.at[s], ssem.at[s]).wait()

    @pl.when(i == 0)
    def _():
        zbuf[...] = jnp.zeros_like(zbuf)
        ybuf[1] = jnp.zeros(ybuf.shape[1:], F32)
        gather_rows(tok0_ref, 0, 0, bm)

    @pl.when(i < n_used)
    def _():
        changed = jnp.logical_or(i == 0, bexp_ref[i] != bexp_ref[jnp.maximum(i - 1, 0)])

        @pl.when(changed)
        def _():
            w13[:, :D_EXPERT] = w1_ref[...].astype(BF16)
            w13[:, D_EXPERT:] = w3_ref[...].astype(BF16)
            w2b[...] = w2_ref[...].astype(BF16)

    def block_step(s):
        wait_gather(s)
        xb = _from_token_rows(xbuf[s]).astype(BF16)
        gate = jnp.dot(xb, w13[:, :D_EXPERT], preferred_element_type=F32)
        gather_rows(tokn_ref, 1 - s, 0, half)
        up = jnp.dot(xb, w13[:, D_EXPERT:], preferred_element_type=F32)
        gather_rows(tokn_ref, 1 - s, half, bm)
        act = (gate * (1.0 / (1.0 + jnp.exp(-gate))) * up).astype(BF16)
        y_lo = jnp.dot(act, w2b[:, :D_MODEL // 2], preferred_element_type=F32)
        scatter_rows(1 - s, 0, half)
        y_hi = jnp.dot(act, w2b[:, D_MODEL // 2:], preferred_element_type=F32)
        scatter_rows(1 - s, half, bm)

        @pl.when(i >= 1)
        def _():
            wait_scatter(s)

        ybuf[s] = _to_token_rows(jnp.concatenate([y_lo, y_hi], axis=1))

    for s in range(2):
        pl.when(jnp.logical_and(i < n_used, slot == s))(functools.partial(block_step, s))

    @pl.when(i == n_used)
    def _():
        wait_gather(slot)
        scatter_rows(1 - slot, 0, bm)
        wait_scatter(0)
        wait_scatter(1)

    @pl.when(jnp.logical_and(i >= n_used, i < nb))
    def _():
        cp = pltpu.make_async_copy(zbuf, out_hbm.at[pl.ds(i * bm, bm)], zsem)
        cp.start()
        cp.wait()


def _experts(x1, plan, w_gate, w_up, w_down, layer):
    bexp, n_used, slot_tok, slot_dst = plan
    nb, _, bm = slot_tok.shape
    spare = (nb * bm + jnp.arange(bm, dtype=jnp.int32)).reshape(1, 1, bm)
    dst_ext = jnp.concatenate([slot_dst, spare], axis=0)
    smem_blk = lambda fn: pl.BlockSpec((1, 1, bm), fn, memory_space=pltpu.SMEM)
    wspec = lambda k, m: pl.BlockSpec((None, None, k, m),
                                      lambda i, be, nu: (layer, be[jnp.minimum(i, nb - 1)], 0, 0))
    grid_spec = pltpu.PrefetchScalarGridSpec(
        num_scalar_prefetch=2,
        grid=(nb + 1,),
        in_specs=[smem_blk(lambda i, be, nu: (0, 0, 0)),
                  smem_blk(lambda i, be, nu: (jnp.minimum(i + 1, nb - 1), 0, 0)),
                  smem_blk(lambda i, be, nu: (jnp.where(i == 0, nb, i - 1), 0, 0)),
                  pl.BlockSpec(memory_space=pl.ANY),
                  wspec(D_MODEL, D_EXPERT), wspec(D_MODEL, D_EXPERT), wspec(D_EXPERT, D_MODEL)],
        out_specs=pl.BlockSpec(memory_space=pl.ANY),
        scratch_shapes=[pltpu.VMEM((2, bm, ROW_TILES, LANES), F32), pltpu.VMEM((2, bm, ROW_TILES, LANES), F32),
                        pltpu.VMEM((bm, ROW_TILES, LANES), F32),
                        pltpu.VMEM((D_MODEL, 2 * D_EXPERT), BF16), pltpu.VMEM((D_EXPERT, D_MODEL), BF16),
                        pltpu.SemaphoreType.DMA((2,)), pltpu.SemaphoreType.DMA((2,)), pltpu.SemaphoreType.DMA(())],
    )
    return pl.pallas_call(
        functools.partial(_expert_kernel, bm=bm, nb=nb),
        grid_spec=grid_spec,
        out_shape=jax.ShapeDtypeStruct(((nb + 1) * bm, ROW_TILES, LANES), F32),
        compiler_params=_cparams(("arbitrary",)),
        name="experts",
    )(bexp, n_used, slot_tok, slot_tok, dst_ext, x1, w_gate, w_up, w_down)


def _post_moe_kernel(x1_ref, y0_ref, y1_ref, route_ref, g_ref, b_ref, o_ref):
    r = route_ref[...]
    f = _from_token_rows(y0_ref[...]) * r[:, 2:3] + _from_token_rows(y1_ref[...]) * r[:, 3:4]
    o_ref[...] = _layer_norm(DN_ALPHA * _from_token_rows(x1_ref[...]) + f, g_ref[...], b_ref[...])


def _post_moe(x1, y, route, g, b):
    n = x1.shape[0]
    tm = TOKEN_TILE
    nt = n // tm
    full = lambda arr: pl.BlockSpec(arr.shape, lambda i: (0,) * arr.ndim)
    rows3 = lambda fn: pl.BlockSpec((tm, ROW_TILES, LANES), fn)
    return pl.pallas_call(
        _post_moe_kernel,
        grid=(nt,),
        in_specs=[rows3(lambda i: (i, 0, 0)),
                  rows3(lambda i: (i, 0, 0)),
                  rows3(lambda i: (nt + i, 0, 0)),
                  pl.BlockSpec((tm, ROUTE_LANES), lambda i: (i, 0)),
                  full(g), full(b)],
        out_specs=pl.BlockSpec((tm, D_MODEL), lambda i: (i, 0)),
        out_shape=jax.ShapeDtypeStruct((n, D_MODEL), F32),
        compiler_params=_cparams(("parallel",)),
        name="post_moe",
    )(x1, y, y, route, g, b)


def _prep_even(w_in, w_uq, w_ukv, w_o):
    c0 = 3 * A_W + B_Q_RANK + B_KV_RANK
    win = jnp.zeros((D_MODEL, EVEN_IN_PAD), F32)
    win = win.at[:, :c0].set(w_in[:, :c0])
    kr0 = EVEN_IN_PAD - LANES + KR_LANE0
    win = win.at[:, kr0:kr0 + B_ROPE].set(w_in[:, c0:c0 + B_ROPE])
    wuq = w_uq.reshape(B_Q_RANK, B_HEADS, B_NOPE + B_ROPE)
    wuq = jnp.pad(wuq, ((0, 0), (0, 0), (0, LANES - B_NOPE - B_ROPE))).reshape(B_Q_RANK, B_HEADS * LANES)
    wukv = w_ukv.reshape(B_KV_RANK, B_HEADS, B_NOPE + B_V)
    wk = jnp.pad(wukv[:, :, :B_NOPE], ((0, 0), (0, 0), (0, LANES - B_NOPE))).reshape(B_KV_RANK, B_HEADS * LANES)
    wv = wukv[:, :, B_NOPE:].reshape(B_KV_RANK, B_HEADS * B_V)
    return win.astype(BF16), wuq.astype(BF16), wk.astype(BF16), wv.astype(BF16), w_o.astype(BF16)


def _prep_router(w_rg, b_rg, w_re, b_re):
    wr = jnp.zeros((D_MODEL, ROUTE_LANES), F32)
    wr = wr.at[:, :N_GROUPS].set(w_rg).at[:, N_GROUPS:N_GROUPS + N_EXPERTS].set(w_re)
    br = jnp.zeros((1, ROUTE_LANES), F32)
    br = br.at[0, :N_GROUPS].set(b_rg).at[0, N_GROUPS:N_GROUPS + N_EXPERTS].set(b_re)
    wr_hi = wr.astype(BF16)
    wr_lo = (wr - wr_hi.astype(F32)).astype(BF16)
    return wr_hi, wr_lo, br


def kernel(x, even_w_in, even_lam_q1, even_lam_k1, even_lam_q2, even_lam_k2, even_subln_g, even_q_norm_g, even_w_uq, even_kv_norm_g, even_w_ukv, even_w_o, odd_w_qkv, odd_rel_bias, odd_w_o, ln1_g, ln1_b, ln2_g, ln2_b, w_router_group, b_router_group, w_router_expert, b_router_expert, w_gate, w_up, w_down):
    bsz, seq, d = x.shape
    n = bsz * seq
    rows = seq // GRID_W
    x2d = x.reshape(n, d)

    a_scale = A_HEAD_DIM ** -0.5 * LOG2E
    b_scale = (B_NOPE + B_ROPE) ** -0.5 * LOG2E
    a_lanes = (0, A_HEAD_DIM)
    rope_tabs = jnp.concatenate([
        _rope_block_tables(seq, A_ROT, a_lanes, a_scale),
        _rope_block_tables(seq, A_ROT, a_lanes, 1.0),
        _rope_block_tables(seq, B_ROPE, (B_NOPE,), b_scale),
        _rope_block_tables(seq, B_ROPE, (KR_LANE0,), 1.0)], axis=0)

    for i in range(DEPTH):
        j = i // 2
        if i % 2 == 0:
            lambda_init = 0.8 - 0.6 * math.exp(-0.3 * i)
            win, wuq, wk, wv, w_o = _prep_even(even_w_in[j], even_w_uq[j], even_w_ukv[j], even_w_o[j])
            lam = (jnp.exp(jnp.sum(even_lam_q1[j] * even_lam_k1[j]))
                   - jnp.exp(jnp.sum(even_lam_q2[j] * even_lam_k2[j])) + lambda_init).reshape(1).astype(F32)
            qkva, qb, kb, vb = _even_proj(x2d, win, wuq, wk, wv, even_q_norm_g[j].reshape(1, -1),
                                          even_kv_norm_g[j].reshape(1, -1), rope_tabs, seq)
            by_seq = lambda t: t.reshape(t.shape[0], bsz, seq, LANES)
            a_out = _diff_attn(by_seq(qkva), lam, even_subln_g[j].reshape(1, -1), 1.0 - lambda_init)
            b_out = _mla_attn(by_seq(qb), by_seq(kb), by_seq(vb))
            acts = [a_out.reshape(-1, n, LANES), b_out.reshape(-1, n, LANES)]
        else:
            qkv = _proj(x2d, odd_w_qkv[j].astype(BF16), C_WIDTH, C_HEAD_DIM ** -0.5 * LOG2E)
            o = _natten(qkv.reshape(-1, bsz, seq, LANES), _natten_tables(odd_rel_bias[j] * LOG2E, rows))
            acts = [o.reshape(-1, n, LANES)]
            w_o = odd_w_o[j].astype(BF16)
        wr_hi, wr_lo, br = _prep_router(w_router_group[i], b_router_group[i], w_router_expert[i], b_router_expert[i])
        x1, route = _post_attn(acts, w_o, x2d, ln1_g[i].reshape(1, -1), ln1_b[i].reshape(1, -1), wr_hi, wr_lo, br)
        plan = _moe_plan(route, n, MOE_BLOCK)
        y = _experts(x1, plan, w_gate, w_up, w_down, i)
        x2d = _post_moe(x1, y, route,ln2_g[i].reshape(1, -1), ln2_b[i].reshape(1, -1))
    return x2d.reshape(bsz, seq, d)
```

```python
import functools
import math

import numpy as np
import jax
import jax.numpy as jnp
from jax import lax
from jax.experimental import pallas as pl
from jax.experimental.pallas import tpu as pltpu

F32 = jnp.float32
BF16 = jnp.bfloat16

D_MODEL = 1024
DEPTH = 4
GRID_W = 64
ROPE_THETA = 500000.0
A_HEADS = 4
A_HEAD_DIM = 64
A_ROT = A_HEAD_DIM // 4
A_W = A_HEADS * 2 * A_HEAD_DIM
B_HEADS = 8
B_NOPE = 64
B_ROPE = 32
B_V = 64
B_Q_RANK = 256
B_KV_RANK = 128
C_HEADS = 16
C_HEAD_DIM = 64
C_WIDTH = C_HEADS * C_HEAD_DIM
NA_ROWS = 8
NA_COLS = 16
N_GROUPS = 4
EXPERTS_PER_GROUP = 8
N_EXPERTS = N_GROUPS * EXPERTS_PER_GROUP
TOP_K = 2
D_EXPERT = 512
DN_ALPHA = (2 * DEPTH) ** 0.25
LN_EPS = 1e-5
RMS_EPS = 1e-6

LANES = 128
ROW_TILES = D_MODEL // LANES
VMEM_LIMIT_BYTES = 52 * 1024 * 1024

EVEN_IN_PAD = 2048
KR_LANE0 = 64
TOKEN_TILE = 1024
PROJ_ROW_SPLIT = 512
ATTN_Q_TILE = 2048
ATTN_ROW_SPLIT = 512
ATTN_JOB_LAG = 1
NA_JOB_LAG = 6
NA_Q_ROWS = 4
NA_Q_TILE = NA_Q_ROWS * GRID_W
NA_BAND_CHUNKS = 3
NA_BATCH_TILE = 8
MOE_BLOCK = 256
NEG_BIG = -1e30
LOG2E = math.log2(math.e)
ROUTE_LANES = LANES


def _cparams(sem):
    return pltpu.CompilerParams(dimension_semantics=sem, vmem_limit_bytes=VMEM_LIMIT_BYTES)


def _rope_block_tables(seq, rot_dim, lane_starts, scale):
    half = rot_dim // 2
    inv_freq = ROPE_THETA ** (-jnp.arange(0, rot_dim, 2, dtype=F32) / rot_dim)
    ang = jnp.arange(seq, dtype=F32)[:, None] * inv_freq[None, :]
    cos, sin = jnp.cos(ang), jnp.sin(ang)
    c = jnp.ones((seq, LANES), F32)
    a = jnp.zeros((seq, LANES), F32)
    b = jnp.zeros((seq, LANES), F32)
    for s0 in lane_starts:
        c = c.at[:, s0:s0 + half].set(cos).at[:, s0 + half:s0 + 2 * half].set(cos)
        a = a.at[:, s0:s0 + half].set(-sin)
        b = b.at[:, s0 + half:s0 + 2 * half].set(sin)
    return jnp.stack([c, a, b]) * scale


def _rope128(x, c, a, b, half):
    return x * c + pltpu.roll(x, LANES - half, 1) * a + pltpu.roll(x, half, 1) * b


def _rms(x, g):
    return x * lax.rsqrt(jnp.mean(x * x, axis=-1, keepdims=True) + RMS_EPS) * g


def _to_token_rows(y):
    pieces = [y[:, c * LANES:(c + 1) * LANES] for c in range(y.shape[1] // LANES)]
    return jnp.swapaxes(jnp.stack(pieces, axis=0), 0, 1)


def _from_token_rows(x3):
    xs = jnp.swapaxes(x3, 0, 1)
    return jnp.concatenate([xs[c] for c in range(xs.shape[0])], axis=1)


def _even_proj_kernel(x_ref, win_ref, wuq_ref, wk_ref, wv_ref, qg_ref, kvg_ref, t_ref,
                      qkva_ref, qb_ref, kb_ref, vb_ref):
    ha = A_ROT // 2
    hb = B_ROPE // 2
    c0 = 3 * A_W
    after = None
    for r0 in range(0, x_ref.shape[0], PROJ_ROW_SPLIT):
        rows = slice(r0, r0 + PROJ_ROW_SPLIT)
        tab = lambda n: t_ref[n, rows]
        xb = x_ref[rows].astype(BF16)
        if after is not None:
            xb = jnp.concatenate([xb[:, :LANES] + _zero_after(after, BF16), xb[:, LANES:]], axis=1)
        h = jnp.dot(xb, win_ref[...], preferred_element_type=F32)
        cq = h[:, c0:c0 + B_Q_RANK]
        after = jnp.mean(cq * cq, axis=-1, keepdims=True)
        for hd in range(A_HEADS):
            lo = hd * LANES
            qkva_ref[hd, rows] = _rope128(h[:, lo:lo + LANES], tab(0), tab(1), tab(2), ha).astype(BF16)
            lo = A_W + hd * LANES
            qkva_ref[A_HEADS + hd, rows] = _rope128(h[:, lo:lo + LANES], tab(3), tab(4), tab(5), ha).astype(BF16)
            lo = 2 * A_W + hd * LANES
            qkva_ref[2 * A_HEADS + hd, rows] = h[:, lo:lo + LANES].astype(BF16)
        cq = cq * lax.rsqrt(after + RMS_EPS) * qg_ref[...]
        qb = jnp.dot(cq.astype(BF16), wuq_ref[...], preferred_element_type=F32)
        ckv = _rms(h[:, c0 + B_Q_RANK:c0 + B_Q_RANK + B_KV_RANK], kvg_ref[...]).astype(BF16)
        kn = jnp.dot(ckv, wk_ref[...], preferred_element_type=F32)
        vb = jnp.dot(ckv, wv_ref[...], preferred_element_type=F32).astype(BF16)
        for p in range(B_HEADS // 2):
            vb_ref[p, rows] = vb[:, p * LANES:(p + 1) * LANES]
        kr = _rope128(h[:, EVEN_IN_PAD - LANES:], tab(9), tab(10), tab(11), hb)
        for hd in range(B_HEADS):
            lo = hd * LANES
            qb_ref[hd, rows] = _rope128(qb[:, lo:lo + LANES], tab(6), tab(7), tab(8), hb).astype(BF16)
            kb_ref[hd, rows] = (kn[:, lo:lo + LANES] + kr).astype(BF16)


def _even_proj(x2d, win, wuq, wk, wv, qg, kvg, tabs, seq):
    n = x2d.shape[0]
    tm = TOKEN_TILE
    assert n % tm == 0 and seq % tm == 0
    spb = seq // tm
    full = lambda shape: pl.BlockSpec(shape, lambda i: (0,) * len(shape))
    row = lambda w: pl.BlockSpec((tm, w), lambda i: (i, 0))
    chunks = lambda c: pl.BlockSpec((c, tm, LANES), lambda i: (0, i, 0))
    return pl.pallas_call(
        _even_proj_kernel,
        grid=(n // tm,),
        in_specs=[row(D_MODEL), full(win.shape), full(wuq.shape), full(wk.shape), full(wv.shape),
                  full(qg.shape), full(kvg.shape),
                  pl.BlockSpec((12, tm, LANES), lambda i: (0, i % spb, 0))],
        out_specs=[chunks(3 * A_HEADS), chunks(B_HEADS), chunks(B_HEADS), chunks(B_HEADS // 2)],
        out_shape=[jax.ShapeDtypeStruct((3 * A_HEADS, n, LANES), BF16),
                   jax.ShapeDtypeStruct((B_HEADS, n, LANES), BF16),
                   jax.ShapeDtypeStruct((B_HEADS, n, LANES), BF16),
                   jax.ShapeDtypeStruct((B_HEADS // 2, n, LANES), BF16)],
        compiler_params=_cparams(("parallel",)),
        name="even_proj",
    )(x2d, win, wuq, wk, wv, qg, kvg, tabs)


_NT = (((1,), (1,)), ((), ()))


def _zero_after(m, dtype):
    u = pltpu.bitcast(jnp.broadcast_to(m, (m.shape[0], LANES)), jnp.uint32)
    return pltpu.bitcast((u >> 16) >> 16, F32).astype(dtype)


def _softmax_pv_jobs(jobs, lag):
    staged = []
    for n, (q, k, v_ones, bias) in enumerate(jobs):
        if n >= lag:
            q = q + _zero_after(staged[n - lag][1], q.dtype)
        s = lax.dot_general(q, k, _NT, preferred_element_type=F32)
        if bias is not None:
            s = s + bias
        staged.append((s, jnp.max(s, axis=-1, keepdims=True), v_ones))
    outs = []
    for s, m, v_ones in staged:
        e = jnp.exp2(s - m).astype(BF16)
        ol = jnp.dot(e, v_ones, preferred_element_type=F32)
        outs.append(ol[:, :LANES] / ol[:, LANES:])
    return outs


def _fill_v_ones(vext_ref, v):
    vext_ref[:, :LANES] = v
    vext_ref[:, LANES:] = jnp.ones(v.shape, v.dtype)


def _diff_attn_kernel(lam_ref, q_ref, k_ref, v_ref, g_ref, o_ref, vext_ref, *, out_scale):
    @pl.when(pl.program_id(2) == 0)
    def _():
        _fill_v_ones(vext_ref, v_ref[0])

    k = k_ref[0]
    v_ones = vext_ref[...]
    tq = q_ref.shape[1]
    lane = lax.broadcasted_iota(jnp.int32, (ATTN_ROW_SPLIT, LANES), 1)
    jobs = []
    for r0 in range(0, tq, ATTN_ROW_SPLIT):
        q = q_ref[0, r0:r0 + ATTN_ROW_SPLIT]
        zero = jnp.zeros_like(q)
        jobs.append((jnp.where(lane < A_HEAD_DIM, q, zero), k, v_ones, None))
        jobs.append((jnp.where(lane >= A_HEAD_DIM, q, zero), k, v_ones, None))
    outs = _softmax_pv_jobs(jobs, ATTN_JOB_LAG)
    for n, r0 in enumerate(range(0, tq, ATTN_ROW_SPLIT)):
        o = outs[2 * n] - lam_ref[0] * outs[2 * n + 1]
        o_ref[0, r0:r0 + ATTN_ROW_SPLIT] = (_rms(o, g_ref[...]) * out_scale).astype(BF16)


def _diff_attn(qkva, lam, subln_g, out_scale):
    _, b, s, _ = qkva.shape
    tq = ATTN_Q_TILE
    blk = lambda rows: (None, 1, rows, LANES)
    return pl.pallas_call(
        functools.partial(_diff_attn_kernel, out_scale=out_scale),
        grid=(b, A_HEADS, s // tq),
        in_specs=[pl.BlockSpec(memory_space=pltpu.SMEM),
                  pl.BlockSpec(blk(tq), lambda bi, h, qi: (h, bi, qi, 0)),
                  pl.BlockSpec(blk(s), lambda bi, h, qi: (A_HEADS + h, bi, 0, 0)),
                  pl.BlockSpec(blk(s), lambda bi, h, qi: (2 * A_HEADS + h, bi, 0, 0)),
                  pl.BlockSpec((1, LANES), lambda bi, h, qi: (0, 0))],
        out_specs=pl.BlockSpec(blk(tq), lambda bi, h, qi: (h, bi, qi, 0)),
        out_shape=jax.ShapeDtypeStruct((A_HEADS, b, s, LANES), BF16),
        scratch_shapes=[pltpu.VMEM((s, 2 * LANES), BF16)],
        compiler_params=_cparams(("parallel", "parallel", "arbitrary")),
        name="diff_attn",
    )(lam, qkva, qkva, qkva, subln_g)


def _mla_attn_kernel(q_ref, k_ref, v_ref, o_ref, vext_ref):
    @pl.when(pl.program_id(2) == 0)
    def _():
        _fill_v_ones(vext_ref, v_ref[0])

    v_ones = vext_ref[...]
    tq = q_ref.shape[2]
    jobs = []
    for r0 in range(0, tq, ATTN_ROW_SPLIT):
        for j in range(2):
            jobs.append((q_ref[j, 0, r0:r0 + ATTN_ROW_SPLIT], k_ref[j, 0], v_ones, None))
    outs = _softmax_pv_jobs(jobs, ATTN_JOB_LAG)
    lane = lax.broadcasted_iota(jnp.int32, outs[0].shape, 1)
    for n, r0 in enumerate(range(0, tq, ATTN_ROW_SPLIT)):
        o_ref[0, r0:r0 + ATTN_ROW_SPLIT] = jnp.where(lane < B_V, outs[2 * n], outs[2 * n + 1]).astype(BF16)


def _mla_attn(qb, kb, vb):
    _, b, s, _ = qb.shape
    tq = ATTN_Q_TILE
    return pl.pallas_call(
        _mla_attn_kernel,
        grid=(b, B_HEADS // 2, s // tq),
        in_specs=[pl.BlockSpec((2, 1, tq, LANES), lambda bi, p, qi: (p, bi, qi, 0)),
                  pl.BlockSpec((2, 1, s, LANES), lambda bi, p, qi: (p, bi, 0, 0)),
                  pl.BlockSpec((None, 1, s, LANES), lambda bi, p, qi: (p, bi, 0, 0))],
        out_specs=pl.BlockSpec((None, 1, tq, LANES), lambda bi, p, qi: (p, bi, qi, 0)),
        out_shape=jax.ShapeDtypeStruct((B_HEADS // 2, b, s, LANES), BF16),
        scratch_shapes=[pltpu.VMEM((s, 2 * LANES), BF16)],
        compiler_params=_cparams(("parallel", "parallel", "arbitrary")),
        name="mla_attn",
    )(qb, kb, vb)


def _proj_kernel(x_ref, w_ref, o_ref, *, q_cols, q_scale):
    acc = jnp.dot(x_ref[...].astype(BF16), w_ref[...], preferred_element_type=F32)
    for c in range(o_ref.shape[0]):
        piece = acc[:, c * LANES:(c + 1) * LANES]
        o_ref[c] = (piece * q_scale if c * LANES < q_cols else piece).astype(o_ref.dtype)


def _proj(x2d, w, q_cols, q_scale):
    n, kdim = x2d.shape
    m = w.shape[1]
    tm = TOKEN_TILE
    return pl.pallas_call(
        functools.partial(_proj_kernel, q_cols=q_cols, q_scale=q_scale),
        grid=(n // tm,),
        in_specs=[pl.BlockSpec((tm, kdim), lambda i: (i, 0)), pl.BlockSpec((kdim, m), lambda i: (0, 0))],
        out_specs=pl.BlockSpec((m // LANES, tm, LANES), lambda i: (0, i, 0)),
        out_shape=jax.ShapeDtypeStruct((m // LANES, n, LANES), BF16),
        compiler_params=_cparams(("parallel",)),
        name="qkv_proj",
    )(x2d, w)


def _na_variant_blocks(rows):
    return (0, 1, rows // NA_Q_ROWS - 1)


def _na_band_start(i, rows):
    return jnp.clip(i - 1, 0, rows // NA_Q_ROWS - NA_BAND_CHUNKS)


def _natten_tables(rel_bias, rows):
    band_rows = NA_BAND_CHUNKS * NA_Q_ROWS
    heads = rel_bias.shape[0]
    qc = np.arange(GRID_W)
    c0 = np.clip(qc - NA_COLS // 2, 0, GRID_W - NA_COLS)
    col_in = (qc[None, :] >= c0[:, None]) & (qc[None, :] < c0[:, None] + NA_COLS)
    col_off = qc[None, :] - qc[:, None] + NA_COLS - 1
    csel = (col_in[:, :, None] & (col_off[:, :, None] == np.arange(2 * NA_COLS - 1))).astype(np.float32)
    rsels, inwins = [], []
    for i_rep in _na_variant_blocks(rows):
        bs = int(np.clip(i_rep - 1, 0, rows // NA_Q_ROWS - NA_BAND_CHUNKS)) * NA_Q_ROWS
        qr = NA_Q_ROWS * i_rep + np.arange(NA_Q_ROWS)
        kr = bs + np.arange(band_rows)
        r0 = np.clip(qr - NA_ROWS // 2, 0, rows - NA_ROWS)
        row_in = (kr[None, :] >= r0[:, None]) & (kr[None, :] < r0[:, None] + NA_ROWS)
        row_off = kr[None, :] - qr[:, None] + NA_ROWS - 1
        rsels.append((row_in[:, :, None] & (row_off[:, :, None] == np.arange(2 * NA_ROWS - 1))).astype(np.float32))
        inwins.append(row_in[:, None, :, None] & col_in[None, :, None, :])
    rsel = np.stack(rsels)
    inwin = np.stack(inwins)
    u = jnp.einsum('hab,vrsa->hvrsb', rel_bias.astype(F32), rsel, precision=lax.Precision.HIGHEST)
    t = jnp.einsum('hvrsb,cdb->hvrcsd', u, csel, precision=lax.Precision.HIGHEST)
    t = jnp.where(inwin[None], t, NEG_BIG)
    return t.reshape(heads, len(rsels), NA_Q_TILE, band_rows * GRID_W)


def _natten_kernel(q_ref, k0_ref, k1_ref, k2_ref, v0_ref, v1_ref, v2_ref, t_ref, o_ref, *, bt):
    lane = lax.broadcasted_iota(jnp.int32, (NA_Q_TILE, LANES), 1)
    lo = lane < C_HEAD_DIM
    jobs = []
    for b in range(bt):
        q = q_ref[b]
        zero = jnp.zeros_like(q)
        k = jnp.concatenate([k0_ref[b], k1_ref[b], k2_ref[b]], axis=0)
        v = jnp.concatenate([v0_ref[b], v1_ref[b], v2_ref[b]], axis=0)
        v_ones = jnp.concatenate([v, jnp.ones(v.shape, v.dtype)], axis=1)
        jobs.append((jnp.where(lo, q, zero), k, v_ones, t_ref[0]))
        jobs.append((jnp.where(lo, zero, q), k, v_ones, t_ref[1]))
    outs = _softmax_pv_jobs(jobs, NA_JOB_LAG)
    for b in range(bt):
        o_ref[b] = jnp.where(lo, outs[2 * b], outs[2 * b + 1]).astype(BF16)


def _natten(qkv, tabs):
    _, b, s, _ = qkv.shape
    rows = s // GRID_W
    nblk = rows // NA_Q_ROWS
    bt = NA_BATCH_TILE if b % NA_BATCH_TILE == 0 else 1
    npair = C_HEADS // 2
    blk = (None, bt, NA_Q_TILE, LANES)

    def band_spec(col0, j):
        return pl.BlockSpec(blk, lambda p, i, bi: (col0 + p, bi, _na_band_start(i, rows) + j, 0))

    def variant(i):
        return jnp.where(i == 0, 0, jnp.where(i == nblk - 1, 2, 1))

    return pl.pallas_call(
        functools.partial(_natten_kernel, bt=bt),
        grid=(npair, nblk, b // bt),
        in_specs=[pl.BlockSpec(blk, lambda p, i, bi: (p, bi, i, 0))]
                 + [band_spec(npair, j) for j in range(NA_BAND_CHUNKS)]
                 + [band_spec(2 * npair, j) for j in range(NA_BAND_CHUNKS)]
                 + [pl.BlockSpec((2, None, NA_Q_TILE, NA_BAND_CHUNKS * NA_Q_TILE),
                                 lambda p, i, bi: (p, variant(i), 0, 0))],
        out_specs=pl.BlockSpec(blk, lambda p, i, bi: (p, bi, i, 0)),
        out_shape=jax.ShapeDtypeStruct((npair, b, s, LANES), BF16),
        compiler_params=_cparams(("parallel", "parallel", "parallel")),
        name="natten",
    )(qkv, qkv, qkv, qkv, qkv, qkv, qkv, tabs)


def _layer_norm(z, g, b):
    mu = jnp.mean(z, axis=-1, keepdims=True)
    zc = z - mu
    var = jnp.mean(zc * zc, axis=-1, keepdims=True)
    return zc * lax.rsqrt(var + LN_EPS) * g + b


def _route(logits):
    lane_i = lax.broadcasted_iota(jnp.int32, logits.shape, 1)
    lane = lane_i.astype(F32)
    big = float(LANES)
    gl = jnp.where(lane_i < N_GROUPS, logits, NEG_BIG)
    gmax = jnp.max(gl, axis=-1, keepdims=True)
    grp = jnp.min(jnp.where(gl == gmax, lane, big), axis=-1, keepdims=True)
    g_w = 1.0 / jnp.sum(jnp.exp(gl - gmax), axis=-1, keepdims=True)
    e_lo = grp * EXPERTS_PER_GROUP + N_GROUPS
    in_grp = jnp.logical_and(lane >= e_lo, lane < e_lo + EXPERTS_PER_GROUP)
    el = jnp.where(in_grp, logits, NEG_BIG)
    v1 = jnp.max(el, axis=-1, keepdims=True)
    i1 = jnp.min(jnp.where(el == v1, lane, big), axis=-1, keepdims=True)
    el2 = jnp.where(lane == i1, NEG_BIG, el)
    v2 = jnp.max(el2, axis=-1, keepdims=True)
    i2 = jnp.min(jnp.where(el2 == v2, lane, big), axis=-1, keepdims=True)
    ex = jnp.exp(v2 - v1)
    den = 1.0 + ex
    gate1 = (1.0 / den) * g_w
    gate2 = (ex / den) * g_w
    rec = jnp.where(lane_i == 0, i1 - N_GROUPS,
                    jnp.where(lane_i == 1, i2 - N_GROUPS,
                              jnp.where(lane_i == 2, gate1, jnp.where(lane_i == 3, gate2, 0.0))))
    return rec


def _post_attn_kernel(*refs, n_act):
    acts = refs[:n_act]
    w_ref, x_ref, g_ref, b_ref, wrh_ref, br_ref, x1_ref, route_ref = refs[n_act:]
    a = jnp.concatenate([a_ref[c] for a_ref in acts for c in range(a_ref.shape[0])], axis=1)
    m = jnp.dot(a, w_ref[...], preferred_element_type=F32)
    y = _layer_norm(DN_ALPHA * x_ref[...] + m, g_ref[...], b_ref[...])
    x1_ref[...] = _to_token_rows(y)
    y_hi = y.astype(BF16)
    y_lo = (y - y_hi.astype(F32)).astype(BF16)
    t = jnp.dot(y_hi, wrh_ref[...], preferred_element_type=F32)
    logits = (t[:, :ROUTE_LANES] + jnp.dot(y_lo, wrh_ref[:, :ROUTE_LANES], preferred_element_type=F32)
              + t[:, ROUTE_LANES:]) + br_ref[...]
    route_ref[...] = _route(logits)


def _post_attn(acts, w_o, x2d, g, b, wrh, br):
    n = x2d.shape[0]
    tm = TOKEN_TILE
    full = lambda arr: pl.BlockSpec(arr.shape, lambda i: (0,) * arr.ndim)
    row = lambda w: pl.BlockSpec((tm, w), lambda i: (i, 0))
    return pl.pallas_call(
        functools.partial(_post_attn_kernel, n_act=len(acts)),
        grid=(n // tm,),
        in_specs=[pl.BlockSpec((a.shape[0], tm, LANES), lambda i: (0, i, 0)) for a in acts]
                 + [full(w_o), row(D_MODEL), full(g), full(b), full(wrh), full(br)],
        out_specs=[pl.BlockSpec((tm, ROW_TILES, LANES), lambda i: (i, 0, 0)), row(ROUTE_LANES)],
        out_shape=[jax.ShapeDtypeStruct((n, ROW_TILES, LANES), F32), jax.ShapeDtypeStruct((n, ROUTE_LANES), F32)],
        compiler_params=_cparams(("parallel",)),
        name="post_attn",
    )(*acts, w_o, x2d, g, b, wrh, br)


def _moe_plan(route, n_tok, bm):
    n_assign = n_tok * TOP_K
    nb = n_assign // bm + N_EXPERTS
    e_flat = route[:, :TOP_K].astype(jnp.int32).reshape(n_assign)
    order = jnp.argsort(e_flat, stable=True).astype(jnp.int32)
    counts = jnp.sum((e_flat[:, None] == jnp.arange(N_EXPERTS, dtype=jnp.int32)[None, :]).astype(jnp.int32), axis=0)
    start = jnp.cumsum(counts) - counts
    padded = (counts + bm - 1) // bm * bm
    pad_end = jnp.cumsum(padded)
    pad_start = pad_end - padded
    n_used = (pad_end[-1] // bm).astype(jnp.int32)
    blk = jnp.arange(nb, dtype=jnp.int32)
    bexp = jnp.sum((pad_end[None, :] <= (blk * bm)[:, None]).astype(jnp.int32), axis=1)
    bexp = jnp.minimum(bexp, N_EXPERTS - 1)
    used = blk < n_used
    bexp = jnp.where(used, bexp, bexp[jnp.maximum(n_used - 1, 0)])
    j = (blk * bm - pad_start[bexp])[:, None] + jnp.arange(bm, dtype=jnp.int32)[None, :]
    cnt = counts[bexp][:, None]
    valid = (j < cnt) & used[:, None]
    a = order[jnp.clip(start[bexp][:, None] + j, 0, n_assign - 1)]
    slot_tok = jnp.where(valid, a // TOP_K, 0)
    pads_before = (pad_start - start)[bexp][:, None]
    pad_dst = jnp.where(used[:, None], n_assign + pads_before + (j - cnt),
                        (blk * bm)[:, None] + jnp.arange(bm, dtype=jnp.int32)[None, :])
    slot_dst = jnp.where(valid, (a % TOP_K) * n_tok + a // TOP_K, pad_dst)
    return (bexp.astype(jnp.int32), n_used.reshape(1), slot_tok.astype(jnp.int32).reshape(nb, 1, bm),
            slot_dst.astype(jnp.int32).reshape(nb, 1, bm))


def _expert_kernel(bexp_ref, nused_ref, tok0_ref, tokn_ref, dstp_ref, x_hbm, w1_ref, w3_ref, w2_ref,
                   out_hbm, xbuf, ybuf, zbuf, w13, w2b, gsem, ssem, zsem, *, bm, nb):
    i = pl.program_id(0)
    n_used = nused_ref[0]
    slot = i % 2
    half = bm // 2

    def gather_rows(tok_ref, s, r0, r1):
        for r in range(r0, r1):
            pltpu.make_async_copy(x_hbm.at[tok_ref[0, 0, r]], xbuf.at[s, r],
                                  gsem.at[s]).start(priority=r % 2)

    def scatter_rows(s, r0, r1):
        for r in range(r0, r1):
            pltpu.make_async_copy(ybuf.at[s, r], out_hbm.at[dstp_ref[0, 0, r]],
                                  ssem.at[s]).start(priority=r % 2)

    def wait_gather(s):
        pltpu.make_async_copy(xbuf.at[s], xbuf.at[s], gsem.at[s]).wait()

    def wait_scatter(s):
        pltpu.make_async_copy(ybuf.at[s], ybuf.at[s], ssem.at[s]).wait()

    @pl.when(i == 0)
    def _():
        zbuf[...] = jnp.zeros_like(zbuf)
        ybuf[1] = jnp.zeros(ybuf.shape[1:], F32)
        gather_rows(tok0_ref, 0, 0, bm)

    @pl.when(i < n_used)
    def _():
        changed = jnp.logical_or(i == 0, bexp_ref[i] != bexp_ref[jnp.maximum(i - 1, 0)])

        @pl.when(changed)
        def _():
            w13[:, :D_EXPERT] = w1_ref[...].astype(BF16)
            w13[:, D_EXPERT:] = w3_ref[...].astype(BF16)
            w2b[...] = w2_ref[...].astype(BF16)

    def block_step(s):
        wait_gather(s)
        xb = _from_token_rows(xbuf[s]).astype(BF16)
        gate = jnp.dot(xb, w13[:, :D_EXPERT], preferred_element_type=F32)
        gather_rows(tokn_ref, 1 - s, 0, half)
        up = jnp.dot(xb, w13[:, D_EXPERT:], preferred_element_type=F32)
        gather_rows(tokn_ref, 1 - s, half, bm)
        act = (gate * (1.0 / (1.0 + jnp.exp(-gate))) * up).astype(BF16)
        y_lo = jnp.dot(act, w2b[:, :D_MODEL // 2], preferred_element_type=F32)
        scatter_rows(1 - s, 0, half)
        y_hi = jnp.dot(act, w2b[:, D_MODEL // 2:], preferred_element_type=F32)
        scatter_rows(1 - s, half, bm)

        @pl.when(i >= 1)
        def _():
            wait_scatter(s)

        ybuf[s] = _to_token_rows(jnp.concatenate([y_lo, y_hi], axis=1))

    for s in range(2):
        pl.when(jnp.logical_and(i < n_used, slot == s))(functools.partial(block_step, s))

    @pl.when(i == n_used)
    def _():
        wait_gather(slot)
        scatter_rows(1 - slot, 0, bm)
        wait_scatter(0)
        wait_scatter(1)

    @pl.when(jnp.logical_and(i >= n_used, i < nb))
    def _():
        cp = pltpu.make_async_copy(zbuf, out_hbm.at[pl.ds(i * bm, bm)], zsem)
        cp.start()
        cp.wait()


def _experts(x1, plan, w_gate, w_up, w_down, layer):
    bexp, n_used, slot_tok, slot_dst = plan
    nb, _, bm = slot_tok.shape
    spare = (nb * bm + jnp.arange(bm, dtype=jnp.int32)).reshape(1, 1, bm)
    dst_ext = jnp.concatenate([slot_dst, spare], axis=0)
    smem_blk = lambda fn: pl.BlockSpec((1, 1, bm), fn, memory_space=pltpu.SMEM)
    wspec = lambda k, m: pl.BlockSpec((None, None, k, m),
                                      lambda i, be, nu: (layer, be[jnp.minimum(i, nb - 1)], 0, 0))
    grid_spec = pltpu.PrefetchScalarGridSpec(
        num_scalar_prefetch=2,
        grid=(nb + 1,),
        in_specs=[smem_blk(lambda i, be, nu: (0, 0, 0)),
                  smem_blk(lambda i, be, nu: (jnp.minimum(i + 1, nb - 1), 0, 0)),
                  smem_blk(lambda i, be, nu: (jnp.where(i == 0, nb, i - 1), 0, 0)),
                  pl.BlockSpec(memory_space=pl.ANY),
                  wspec(D_MODEL, D_EXPERT), wspec(D_MODEL, D_EXPERT), wspec(D_EXPERT, D_MODEL)],
        out_specs=pl.BlockSpec(memory_space=pl.ANY),
        scratch_shapes=[pltpu.VMEM((2, bm, ROW_TILES, LANES), F32), pltpu.VMEM((2, bm, ROW_TILES, LANES), F32),
                        pltpu.VMEM((bm, ROW_TILES, LANES), F32),
                        pltpu.VMEM((D_MODEL, 2 * D_EXPERT), BF16), pltpu.VMEM((D_EXPERT, D_MODEL), BF16),
                        pltpu.SemaphoreType.DMA((2,)), pltpu.SemaphoreType.DMA((2,)), pltpu.SemaphoreType.DMA(())],
    )
    return pl.pallas_call(
        functools.partial(_expert_kernel, bm=bm, nb=nb),
        grid_spec=grid_spec,
        out_shape=jax.ShapeDtypeStruct(((nb + 1) * bm, ROW_TILES, LANES), F32),
        compiler_params=_cparams(("arbitrary",)),
        name="experts",
    )(bexp, n_used, slot_tok, slot_tok, dst_ext, x1, w_gate, w_up, w_down)


def _post_moe_kernel(x1_ref, y0_ref, y1_ref, route_ref, g_ref, b_ref, o_ref):
    r = route_ref[...]
    f = _from_token_rows(y0_ref[...]) * r[:, 2:3] + _from_token_rows(y1_ref[...]) * r[:, 3:4]
    o_ref[...] = _layer_norm(DN_ALPHA * _from_token_rows(x1_ref[...]) + f, g_ref[...], b_ref[...])


def _post_moe(x1, y, route, g, b):
    n = x1.shape[0]
    tm = TOKEN_TILE
    nt = n // tm
    full = lambda arr: pl.BlockSpec(arr.shape, lambda i: (0,) * arr.ndim)
    rows3 = lambda fn: pl.BlockSpec((tm, ROW_TILES, LANES), fn)
    return pl.pallas_call(
        _post_moe_kernel,
        grid=(nt,),
        in_specs=[rows3(lambda i: (i, 0, 0)),
                  rows3(lambda i: (i, 0, 0)),
                  rows3(lambda i: (nt + i, 0, 0)),
                  pl.BlockSpec((tm, ROUTE_LANES), lambda i: (i, 0)),
                  full(g), full(b)],
        out_specs=pl.BlockSpec((tm, D_MODEL), lambda i: (i, 0)),
        out_shape=jax.ShapeDtypeStruct((n, D_MODEL), F32),
        compiler_params=_cparams(("parallel",)),
        name="post_moe",
    )(x1, y, y, route, g, b)


def _prep_even(w_in, w_uq, w_ukv, w_o):
    c0 = 3 * A_W + B_Q_RANK + B_KV_RANK
    win = jnp.zeros((D_MODEL, EVEN_IN_PAD), F32)
    win = win.at[:, :c0].set(w_in[:, :c0])
    kr0 = EVEN_IN_PAD - LANES + KR_LANE0
    win = win.at[:, kr0:kr0 + B_ROPE].set(w_in[:, c0:c0 + B_ROPE])
    wuq = w_uq.reshape(B_Q_RANK, B_HEADS, B_NOPE + B_ROPE)
    wuq = jnp.pad(wuq, ((0, 0), (0, 0), (0, LANES - B_NOPE - B_ROPE))).reshape(B_Q_RANK, B_HEADS * LANES)
    wukv = w_ukv.reshape(B_KV_RANK, B_HEADS, B_NOPE + B_V)
    wk = jnp.pad(wukv[:, :, :B_NOPE], ((0, 0), (0, 0), (0, LANES - B_NOPE))).reshape(B_KV_RANK, B_HEADS * LANES)
    wv = wukv[:, :, B_NOPE:].reshape(B_KV_RANK, B_HEADS * B_V)
    return win.astype(BF16), wuq.astype(BF16), wk.astype(BF16), wv.astype(BF16), w_o.astype(BF16)


def _prep_router(w_rg, b_rg, w_re, b_re):
    wr = jnp.zeros((D_MODEL, ROUTE_LANES), F32)
    wr = wr.at[:, :N_GROUPS].set(w_rg).at[:, N_GROUPS:N_GROUPS + N_EXPERTS].set(w_re)
    br = jnp.zeros((1, ROUTE_LANES), F32)
    br = br.at[0, :N_GROUPS].set(b_rg).at[0, N_GROUPS:N_GROUPS + N_EXPERTS].set(b_re)
    wr_hi = wr.astype(BF16)
    wr_lo = (wr - wr_hi.astype(F32)).astype(BF16)
    return jnp.concatenate([wr_hi, wr_lo], axis=1), br


def kernel(x, even_w_in, even_lam_q1, even_lam_k1, even_lam_q2, even_lam_k2, even_subln_g, even_q_norm_g, even_w_uq, even_kv_norm_g, even_w_ukv, even_w_o, odd_w_qkv, odd_rel_bias, odd_w_o, ln1_g, ln1_b, ln2_g, ln2_b, w_router_group, b_router_group, w_router_expert, b_router_expert, w_gate, w_up, w_down):
    bsz, seq, d = x.shape
    n = bsz * seq
    rows = seq // GRID_W
    x2d = x.reshape(n, d)

    a_scale = A_HEAD_DIM ** -0.5 * LOG2E
    b_scale = (B_NOPE + B_ROPE) ** -0.5 * LOG2E
    a_lanes = (0, A_HEAD_DIM)
    rope_tabs = jnp.concatenate([
        _rope_block_tables(seq, A_ROT, a_lanes, a_scale),
        _rope_block_tables(seq, A_ROT, a_lanes, 1.0),
        _rope_block_tables(seq, B_ROPE, (B_NOPE,), b_scale),
        _rope_block_tables(seq, B_ROPE, (KR_LANE0,), 1.0)], axis=0)

    for i in range(DEPTH):
        j = i // 2
        if i % 2 == 0:
            lambda_init = 0.8 - 0.6 * math.exp(-0.3 * i)
            win, wuq, wk, wv, w_o = _prep_even(even_w_in[j], even_w_uq[j], even_w_ukv[j], even_w_o[j])
            lam = (jnp.exp(jnp.sum(even_lam_q1[j] * even_lam_k1[j]))
                   - jnp.exp(jnp.sum(even_lam_q2[j] * even_lam_k2[j])) + lambda_init).reshape(1).astype(F32)
            qkva, qb, kb, vb = _even_proj(x2d, win, wuq, wk, wv, even_q_norm_g[j].reshape(1, -1),
                                          even_kv_norm_g[j].reshape(1, -1), rope_tabs, seq)
            by_seq = lambda t: t.reshape(t.shape[0], bsz, seq, LANES)
            a_out = _diff_attn(by_seq(qkva), lam, even_subln_g[j].reshape(1, -1), 1.0 - lambda_init)
            b_out = _mla_attn(by_seq(qb), by_seq(kb), by_seq(vb))
            acts = [a_out.reshape(-1, n, LANES), b_out.reshape(-1, n, LANES)]
        else:
            qkv = _proj(x2d, odd_w_qkv[j].astype(BF16), C_WIDTH, C_HEAD_DIM ** -0.5 * LOG2E)
            o = _natten(qkv.reshape(-1, bsz, seq, LANES), _natten_tables(odd_rel_bias[j] * LOG2E, rows))
            acts = [o.reshape(-1, n, LANES)]
            w_o = odd_w_o[j].astype(BF16)
        wr, br = _prep_router(w_router_group[i], b_router_group[i], w_router_expert[i], b_router_expert[i])
        x1, route = _post_attn(acts, w_o, x2d, ln1_g[i].reshape(1, -1), ln1_b[i].reshape(1, -1), wr, br)
        plan = _moe_plan(route, n, MOE_BLOCK)
        y = _experts(x1, plan, w_gate, w_up, w_down, i)
        x2d = _post_moe(x1, y, route,ln2_g[i].reshape(1, -1), ln2_b[i].reshape(1, -1))
    return x2d.reshape(bsz, seq, d)
```

```python
import functools
import math

import numpy as np
import jax
import jax.numpy as jnp
from jax import lax
from jax.experimental import pallas as pl
from jax.experimental.pallas import tpu as pltpu

F32 = jnp.float32
BF16 = jnp.bfloat16

D_MODEL = 1024
DEPTH = 4
GRID_W = 64
ROPE_THETA = 500000.0
A_HEADS = 4
A_HEAD_DIM = 64
A_ROT = A_HEAD_DIM // 4
A_W = A_HEADS * 2 * A_HEAD_DIM
B_HEADS = 8
B_NOPE = 64
B_ROPE = 32
B_V = 64
B_Q_RANK = 256
B_KV_RANK = 128
C_HEADS = 16
C_HEAD_DIM = 64
C_WIDTH = C_HEADS * C_HEAD_DIM
NA_ROWS = 8
NA_COLS = 16
N_GROUPS = 4
EXPERTS_PER_GROUP = 8
N_EXPERTS = N_GROUPS * EXPERTS_PER_GROUP
TOP_K = 2
D_EXPERT = 512
DN_ALPHA = (2 * DEPTH) ** 0.25
LN_EPS = 1e-5
RMS_EPS = 1e-6

LANES = 128
ROW_TILES = D_MODEL // LANES
VMEM_LIMIT_BYTES = 52 * 1024 * 1024

EVEN_IN_PAD = 2048
KR_LANE0 = 64
TOKEN_TILE = 1024
PROJ_ROW_SPLIT = 512
ATTN_Q_TILE = 2048
ATTN_ROW_SPLIT = 1024
ATTN_JOB_LAG = 1
NA_JOB_LAG = 8
NA_Q_ROWS = 4
NA_Q_TILE = NA_Q_ROWS * GRID_W
NA_BAND_CHUNKS = 3
NA_BATCH_TILE = 16
MOE_BLOCK = 256
NEG_BIG = -1e30
LOG2E = math.log2(math.e)
ROUTE_LANES = LANES


def _cparams(sem):
    return pltpu.CompilerParams(dimension_semantics=sem, vmem_limit_bytes=VMEM_LIMIT_BYTES)


def _rope_block_tables(seq, rot_dim, lane_starts, scale):
    half = rot_dim // 2
    inv_freq = ROPE_THETA ** (-jnp.arange(0, rot_dim, 2, dtype=F32) / rot_dim)
    ang = jnp.arange(seq, dtype=F32)[:, None] * inv_freq[None, :]
    cos, sin = jnp.cos(ang), jnp.sin(ang)
    c = jnp.ones((seq, LANES), F32)
    a = jnp.zeros((seq, LANES), F32)
    b = jnp.zeros((seq, LANES), F32)
    for s0 in lane_starts:
        c = c.at[:, s0:s0 + half].set(cos).at[:, s0 + half:s0 + 2 * half].set(cos)
        a = a.at[:, s0:s0 + half].set(-sin)
        b = b.at[:, s0 + half:s0 + 2 * half].set(sin)
    return jnp.stack([c, a, b]) * scale


def _rope128(x, c, a, b, half):
    return x * c + pltpu.roll(x, LANES - half, 1) * a + pltpu.roll(x, half, 1) * b


def _rms(x, g):
    return x * lax.rsqrt(jnp.mean(x * x, axis=-1, keepdims=True) + RMS_EPS) * g


def _to_token_rows(y):
    pieces = [y[:, c * LANES:(c + 1) * LANES] for c in range(y.shape[1] // LANES)]
    return jnp.swapaxes(jnp.stack(pieces, axis=0), 0, 1)


def _from_token_rows(x3):
    xs = jnp.swapaxes(x3, 0, 1)
    return jnp.concatenate([xs[c] for c in range(xs.shape[0])], axis=1)


def _even_proj_kernel(x_ref, win_ref, wuq_ref, wk_ref, wv_ref, qg_ref, kvg_ref, t_ref,
                      qkva_ref, qb_ref, kb_ref, vb_ref):
    ha = A_ROT // 2
    hb = B_ROPE // 2
    c0 = 3 * A_W
    after = None
    for r0 in range(0, x_ref.shape[0], PROJ_ROW_SPLIT):
        rows = slice(r0, r0 + PROJ_ROW_SPLIT)
        tab = lambda n: t_ref[n, rows]
        xb = x_ref[rows].astype(BF16)
        if after is not None:
            xb = jnp.concatenate([xb[:, :LANES] + _zero_after(after, BF16), xb[:, LANES:]], axis=1)
        h = jnp.dot(xb, win_ref[...], preferred_element_type=F32)
        cq = h[:, c0:c0 + B_Q_RANK]
        after = jnp.mean(cq * cq, axis=-1, keepdims=True)
        for hd in range(A_HEADS):
            lo = hd * LANES
            qkva_ref[hd, rows] = _rope128(h[:, lo:lo + LANES], tab(0), tab(1), tab(2), ha).astype(BF16)
            lo = A_W + hd * LANES
            qkva_ref[A_HEADS + hd, rows] = _rope128(h[:, lo:lo + LANES], tab(3), tab(4), tab(5), ha).astype(BF16)
            lo = 2 * A_W + hd * LANES
            qkva_ref[2 * A_HEADS + hd, rows] = h[:, lo:lo + LANES].astype(BF16)
        cq = cq * lax.rsqrt(after + RMS_EPS) * qg_ref[...]
        qb = jnp.dot(cq.astype(BF16), wuq_ref[...], preferred_element_type=F32)
        ckv = _rms(h[:, c0 + B_Q_RANK:c0 + B_Q_RANK + B_KV_RANK], kvg_ref[...]).astype(BF16)
        kn = jnp.dot(ckv, wk_ref[...], preferred_element_type=F32)
        vb = jnp.dot(ckv, wv_ref[...], preferred_element_type=F32).astype(BF16)
        for p in range(B_HEADS // 2):
            vb_ref[p, rows] = vb[:, p * LANES:(p + 1) * LANES]
        kr = _rope128(h[:, EVEN_IN_PAD - LANES:], tab(9), tab(10), tab(11), hb)
        for hd in range(B_HEADS):
            lo = hd * LANES
            qb_ref[hd, rows] = _rope128(qb[:, lo:lo + LANES], tab(6), tab(7), tab(8), hb).astype(BF16)
            kb_ref[hd, rows] = (kn[:, lo:lo + LANES] + kr).astype(BF16)


def _even_proj(x2d, win, wuq, wk, wv, qg, kvg, tabs, seq):
    n = x2d.shape[0]
    tm = TOKEN_TILE
    assert n % tm == 0 and seq % tm == 0
    spb = seq // tm
    full = lambda shape: pl.BlockSpec(shape, lambda i: (0,) * len(shape))
    row = lambda w: pl.BlockSpec((tm, w), lambda i: (i, 0))
    chunks = lambda c: pl.BlockSpec((c, tm, LANES), lambda i: (0, i, 0))
    return pl.pallas_call(
        _even_proj_kernel,
        grid=(n // tm,),
        in_specs=[row(D_MODEL), full(win.shape), full(wuq.shape), full(wk.shape), full(wv.shape),
                  full(qg.shape), full(kvg.shape),
                  pl.BlockSpec((12, tm, LANES), lambda i: (0, i % spb, 0))],
        out_specs=[chunks(3 * A_HEADS), chunks(B_HEADS), chunks(B_HEADS), chunks(B_HEADS // 2)],
        out_shape=[jax.ShapeDtypeStruct((3 * A_HEADS, n, LANES), BF16),
                   jax.ShapeDtypeStruct((B_HEADS, n, LANES), BF16),
                   jax.ShapeDtypeStruct((B_HEADS, n, LANES), BF16),
                   jax.ShapeDtypeStruct((B_HEADS // 2, n, LANES), BF16)],
        compiler_params=_cparams(("parallel",)),
        name="even_proj",
    )(x2d, win, wuq, wk, wv, qg, kvg, tabs)


_NT = (((1,), (1,)), ((), ()))


def _zero_after(m, dtype):
    u = pltpu.bitcast(jnp.broadcast_to(m, (m.shape[0], LANES)), jnp.uint32)
    return pltpu.bitcast((u >> 16) >> 16, F32).astype(dtype)


def _softmax_pv_jobs(jobs, lag):
    staged = []
    for n, (q, k, v_ones, bias) in enumerate(jobs):
        if n >= lag:
            q = q + _zero_after(staged[n - lag][1], q.dtype)
        s = lax.dot_general(q, k, _NT, preferred_element_type=F32)
        if bias is not None:
            s = s + bias
        staged.append((s, jnp.max(s, axis=-1, keepdims=True), v_ones))
    outs = []
    for s, m, v_ones in staged:
        e = jnp.exp2(s - m).astype(BF16)
        ol = jnp.dot(e, v_ones, preferred_element_type=F32)
        outs.append(ol[:, :LANES] / ol[:, LANES:])
    return outs


def _fill_v_ones(vext_ref, v):
    vext_ref[:, :LANES] = v
    vext_ref[:, LANES:] = jnp.ones(v.shape, v.dtype)


def _diff_attn_kernel(lam_ref, q_ref, k_ref, v_ref, g_ref, o_ref, vext_ref, *, out_scale):
    @pl.when(pl.program_id(2) == 0)
    def _():
        _fill_v_ones(vext_ref, v_ref[0])

    k = k_ref[0]
    v_ones = vext_ref[...]
    tq = q_ref.shape[1]
    lane = lax.broadcasted_iota(jnp.int32, (ATTN_ROW_SPLIT, LANES), 1)
    jobs = []
    for r0 in range(0, tq, ATTN_ROW_SPLIT):
        q = q_ref[0, r0:r0 + ATTN_ROW_SPLIT]
        zero = jnp.zeros_like(q)
        jobs.append((jnp.where(lane < A_HEAD_DIM, q, zero), k, v_ones, None))
        jobs.append((jnp.where(lane >= A_HEAD_DIM, q, zero), k, v_ones, None))
    outs = _softmax_pv_jobs(jobs, ATTN_JOB_LAG)
    for n, r0 in enumerate(range(0, tq, ATTN_ROW_SPLIT)):
        o = outs[2 * n] - lam_ref[0] * outs[2 * n + 1]
        o_ref[0, r0:r0 + ATTN_ROW_SPLIT] = (_rms(o, g_ref[...]) * out_scale).astype(BF16)


def _diff_attn(qkva, lam, subln_g, out_scale):
    _, b, s, _ = qkva.shape
    tq = ATTN_Q_TILE
    blk = lambda rows: (None, 1, rows, LANES)
    return pl.pallas_call(
        functools.partial(_diff_attn_kernel, out_scale=out_scale),
        grid=(b, A_HEADS, s // tq),
        in_specs=[pl.BlockSpec(memory_space=pltpu.SMEM),
                  pl.BlockSpec(blk(tq), lambda bi, h, qi: (h, bi, qi, 0)),
                  pl.BlockSpec(blk(s), lambda bi, h, qi: (A_HEADS + h, bi, 0, 0)),
                  pl.BlockSpec(blk(s), lambda bi, h, qi: (2 * A_HEADS + h, bi, 0, 0)),
                  pl.BlockSpec((1, LANES), lambda bi, h, qi: (0, 0))],
        out_specs=pl.BlockSpec(blk(tq), lambda bi, h, qi: (h, bi, qi, 0)),
        out_shape=jax.ShapeDtypeStruct((A_HEADS, b, s, LANES), BF16),
        scratch_shapes=[pltpu.VMEM((s, 2 * LANES), BF16)],
        compiler_params=_cparams(("parallel", "parallel", "arbitrary")),
        name="diff_attn",
    )(lam, qkva, qkva, qkva, subln_g)


def _mla_attn_kernel(q_ref, k_ref, v_ref, o_ref, vext_ref):
    @pl.when(pl.program_id(2) == 0)
    def _():
        _fill_v_ones(vext_ref, v_ref[0])

    v_ones = vext_ref[...]
    tq = q_ref.shape[2]
    jobs = []
    for r0 in range(0, tq, ATTN_ROW_SPLIT):
        for j in range(2):
            jobs.append((q_ref[j, 0, r0:r0 + ATTN_ROW_SPLIT], k_ref[j, 0], v_ones, None))
    outs = _softmax_pv_jobs(jobs, ATTN_JOB_LAG)
    lane = lax.broadcasted_iota(jnp.int32, outs[0].shape, 1)
    for n, r0 in enumerate(range(0, tq, ATTN_ROW_SPLIT)):
        o_ref[0, r0:r0 + ATTN_ROW_SPLIT] = jnp.where(lane < B_V, outs[2 * n], outs[2 * n + 1]).astype(BF16)


def _mla_attn(qb, kb, vb):
    _, b, s, _ = qb.shape
    tq = ATTN_Q_TILE
    return pl.pallas_call(
        _mla_attn_kernel,
        grid=(b, B_HEADS // 2, s // tq),
        in_specs=[pl.BlockSpec((2, 1, tq, LANES), lambda bi, p, qi: (p, bi, qi, 0)),
                  pl.BlockSpec((2, 1, s, LANES), lambda bi, p, qi: (p, bi, 0, 0)),
                  pl.BlockSpec((None, 1, s, LANES), lambda bi, p, qi: (p, bi, 0, 0))],
        out_specs=pl.BlockSpec((None, 1, tq, LANES), lambda bi, p, qi: (p, bi, qi, 0)),
        out_shape=jax.ShapeDtypeStruct((B_HEADS // 2, b, s, LANES), BF16),
        scratch_shapes=[pltpu.VMEM((s, 2 * LANES), BF16)],
        compiler_params=_cparams(("parallel", "parallel", "arbitrary")),
        name="mla_attn",
    )(qb, kb, vb)


def _proj_kernel(x_ref, w_ref, o_ref, *, q_cols, q_scale):
    acc = jnp.dot(x_ref[...].astype(BF16), w_ref[...], preferred_element_type=F32)
    for c in range(o_ref.shape[0]):
        piece = acc[:, c * LANES:(c + 1) * LANES]
        o_ref[c] = (piece * q_scale if c * LANES < q_cols else piece).astype(o_ref.dtype)


def _proj(x2d, w, q_cols, q_scale):
    n, kdim = x2d.shape
    m = w.shape[1]
    tm = TOKEN_TILE
    return pl.pallas_call(
        functools.partial(_proj_kernel, q_cols=q_cols, q_scale=q_scale),
        grid=(n // tm,),
        in_specs=[pl.BlockSpec((tm, kdim), lambda i: (i, 0)), pl.BlockSpec((kdim, m), lambda i: (0, 0))],
        out_specs=pl.BlockSpec((m // LANES, tm, LANES), lambda i: (0, i, 0)),
        out_shape=jax.ShapeDtypeStruct((m // LANES, n, LANES), BF16),
        compiler_params=_cparams(("parallel",)),
        name="qkv_proj",
    )(x2d, w)


def _na_variant_blocks(rows):
    return (0, 1, rows // NA_Q_ROWS - 1)


def _na_band_start(i, rows):
    return jnp.clip(i - 1, 0, rows // NA_Q_ROWS - NA_BAND_CHUNKS)


def _natten_tables(rel_bias, rows):
    band_rows = NA_BAND_CHUNKS * NA_Q_ROWS
    heads = rel_bias.shape[0]
    qc = np.arange(GRID_W)
    c0 = np.clip(qc - NA_COLS // 2, 0, GRID_W - NA_COLS)
    col_in = (qc[None, :] >= c0[:, None]) & (qc[None, :] < c0[:, None] + NA_COLS)
    col_off = qc[None, :] - qc[:, None] + NA_COLS - 1
    csel = (col_in[:, :, None] & (col_off[:, :, None] == np.arange(2 * NA_COLS - 1))).astype(np.float32)
    rsels, inwins = [], []
    for i_rep in _na_variant_blocks(rows):
        bs = int(np.clip(i_rep - 1, 0, rows // NA_Q_ROWS - NA_BAND_CHUNKS)) * NA_Q_ROWS
        qr = NA_Q_ROWS * i_rep + np.arange(NA_Q_ROWS)
        kr = bs + np.arange(band_rows)
        r0 = np.clip(qr - NA_ROWS // 2, 0, rows - NA_ROWS)
        row_in = (kr[None, :] >= r0[:, None]) & (kr[None, :] < r0[:, None] + NA_ROWS)
        row_off = kr[None, :] - qr[:, None] + NA_ROWS - 1
        rsels.append((row_in[:, :, None] & (row_off[:, :, None] == np.arange(2 * NA_ROWS - 1))).astype(np.float32))
        inwins.append(row_in[:, None, :, None] & col_in[None, :, None, :])
    rsel = np.stack(rsels)
    inwin = np.stack(inwins)
    u = jnp.einsum('hab,vrsa->hvrsb', rel_bias.astype(F32), rsel, precision=lax.Precision.HIGHEST)
    t = jnp.einsum('hvrsb,cdb->hvrcsd', u, csel, precision=lax.Precision.HIGHEST)
    t = jnp.where(inwin[None], t, NEG_BIG)
    return t.reshape(heads, len(rsels), NA_Q_TILE, band_rows * GRID_W)


def _natten_kernel(q_ref, k0_ref, k1_ref, k2_ref, v0_ref, v1_ref, v2_ref, t_ref, o_ref, *, bt):
    lane = lax.broadcasted_iota(jnp.int32, (NA_Q_TILE, LANES), 1)
    lo = lane < C_HEAD_DIM
    jobs = []
    for b in range(bt):
        q = q_ref[b]
        zero = jnp.zeros_like(q)
        k = jnp.concatenate([k0_ref[b], k1_ref[b], k2_ref[b]], axis=0)
        v = jnp.concatenate([v0_ref[b], v1_ref[b], v2_ref[b]], axis=0)
        v_ones = jnp.concatenate([v, jnp.ones(v.shape, v.dtype)], axis=1)
        jobs.append((jnp.where(lo, q, zero), k, v_ones, t_ref[0]))
        jobs.append((jnp.where(lo, zero, q), k, v_ones, t_ref[1]))
    outs = _softmax_pv_jobs(jobs, NA_JOB_LAG)
    for b in range(bt):
        o_ref[b] = jnp.where(lo, outs[2 * b], outs[2 * b + 1]).astype(BF16)


def _natten(qkv, tabs):
    _, b, s, _ = qkv.shape
    rows = s // GRID_W
    nblk = rows // NA_Q_ROWS
    bt = NA_BATCH_TILE if b % NA_BATCH_TILE == 0 else 1
    npair = C_HEADS // 2
    blk = (None, bt, NA_Q_TILE, LANES)

    def band_spec(col0, j):
        return pl.BlockSpec(blk, lambda p, i, bi: (col0 + p, bi, _na_band_start(i, rows) + j, 0))

    def variant(i):
        return jnp.where(i == 0, 0, jnp.where(i == nblk - 1, 2, 1))

    return pl.pallas_call(
        functools.partial(_natten_kernel, bt=bt),
        grid=(npair, nblk, b // bt),
        in_specs=[pl.BlockSpec(blk, lambda p, i, bi: (p, bi, i, 0))]
                 + [band_spec(npair, j) for j in range(NA_BAND_CHUNKS)]
                 + [band_spec(2 * npair, j) for j in range(NA_BAND_CHUNKS)]
                 + [pl.BlockSpec((2, None, NA_Q_TILE, NA_BAND_CHUNKS * NA_Q_TILE),
                                 lambda p, i, bi: (p, variant(i), 0, 0))],
        out_specs=pl.BlockSpec(blk, lambda p, i, bi: (p, bi, i, 0)),
        out_shape=jax.ShapeDtypeStruct((npair, b, s, LANES), BF16),
        compiler_params=_cparams(("parallel", "parallel", "parallel")),
        name="natten",
    )(qkv, qkv, qkv, qkv, qkv, qkv, qkv, tabs)


def _layer_norm(z, g, b):
    mu = jnp.mean(z, axis=-1, keepdims=True)
    zc = z - mu
    var = jnp.mean(zc * zc, axis=-1, keepdims=True)
    return zc * lax.rsqrt(var + LN_EPS) * g + b


def _route(logits):
    lane_i = lax.broadcasted_iota(jnp.int32, logits.shape, 1)
    lane = lane_i.astype(F32)
    big = float(LANES)
    gl = jnp.where(lane_i < N_GROUPS, logits, NEG_BIG)
    gmax = jnp.max(gl, axis=-1, keepdims=True)
    grp = jnp.min(jnp.where(gl == gmax, lane, big), axis=-1, keepdims=True)
    g_w = 1.0 / jnp.sum(jnp.exp(gl - gmax), axis=-1, keepdims=True)
    e_lo = grp * EXPERTS_PER_GROUP + N_GROUPS
    in_grp = jnp.logical_and(lane >= e_lo, lane < e_lo + EXPERTS_PER_GROUP)
    el = jnp.where(in_grp, logits, NEG_BIG)
    v1 = jnp.max(el, axis=-1, keepdims=True)
    i1 = jnp.min(jnp.where(el == v1, lane, big), axis=-1, keepdims=True)
    el2 = jnp.where(lane == i1, NEG_BIG, el)
    v2 = jnp.max(el2, axis=-1, keepdims=True)
    i2 = jnp.min(jnp.where(el2 == v2, lane, big), axis=-1, keepdims=True)
    ex = jnp.exp(v2 - v1)
    den = 1.0 + ex
    gate1 = (1.0 / den) * g_w
    gate2 = (ex / den) * g_w
    rec = jnp.where(lane_i == 0, i1 - N_GROUPS,
                    jnp.where(lane_i == 1, i2 - N_GROUPS,
                              jnp.where(lane_i == 2, gate1, jnp.where(lane_i == 3, gate2, 0.0))))
    return rec


def _post_attn_kernel(*refs, n_act):
    acts = refs[:n_act]
    w_ref, x_ref, g_ref, b_ref, wrh_ref, br_ref, x1_ref, route_ref = refs[n_act:]
    a = jnp.concatenate([a_ref[c] for a_ref in acts for c in range(a_ref.shape[0])], axis=1)
    m = jnp.dot(a, w_ref[...], preferred_element_type=F32)
    y = _layer_norm(DN_ALPHA * x_ref[...] + m, g_ref[...], b_ref[...])
    x1_ref[...] = _to_token_rows(y)
    y_hi = y.astype(BF16)
    y_lo = (y - y_hi.astype(F32)).astype(BF16)
    t = jnp.dot(y_hi, wrh_ref[...], preferred_element_type=F32)
    logits = (t[:, :ROUTE_LANES] + jnp.dot(y_lo, wrh_ref[:, :ROUTE_LANES], preferred_element_type=F32)
              + t[:, ROUTE_LANES:]) + br_ref[...]
    route_ref[...] = _route(logits)


def _post_attn(acts, w_o, x2d, g, b, wrh, br):
    n = x2d.shape[0]
    tm = TOKEN_TILE
    full = lambda arr: pl.BlockSpec(arr.shape, lambda i: (0,) * arr.ndim)
    row = lambda w: pl.BlockSpec((tm, w), lambda i: (i, 0))
    return pl.pallas_call(
        functools.partial(_post_attn_kernel, n_act=len(acts)),
        grid=(n // tm,),
        in_specs=[pl.BlockSpec((a.shape[0], tm, LANES), lambda i: (0, i, 0)) for a in acts]
                 + [full(w_o), row(D_MODEL), full(g), full(b), full(wrh), full(br)],
        out_specs=[pl.BlockSpec((tm, ROW_TILES, LANES), lambda i: (i, 0, 0)), row(ROUTE_LANES)],
        out_shape=[jax.ShapeDtypeStruct((n, ROW_TILES, LANES), F32), jax.ShapeDtypeStruct((n, ROUTE_LANES), F32)],
        compiler_params=_cparams(("parallel",)),
        name="post_attn",
    )(*acts, w_o, x2d, g, b, wrh, br)


def _moe_plan(route, n_tok, bm):
    n_assign = n_tok * TOP_K
    nb = n_assign // bm + N_EXPERTS
    e_flat = route[:, :TOP_K].astype(jnp.int32).reshape(n_assign)
    assert N_EXPERTS * n_assign < 2 ** 31
    order = jnp.sort(e_flat * n_assign + jnp.arange(n_assign, dtype=jnp.int32)) % n_assign
    counts = jnp.sum((e_flat[:, None] == jnp.arange(N_EXPERTS, dtype=jnp.int32)[None, :]).astype(jnp.int32), axis=0)
    start = jnp.cumsum(counts) - counts
    padded = (counts + bm - 1) // bm * bm
    pad_end = jnp.cumsum(padded)
    pad_start = pad_end - padded
    n_used = (pad_end[-1] // bm).astype(jnp.int32)
    blk = jnp.arange(nb, dtype=jnp.int32)
    bexp = jnp.sum((pad_end[None, :] <= (blk * bm)[:, None]).astype(jnp.int32), axis=1)
    bexp = jnp.minimum(bexp, N_EXPERTS - 1)
    used = blk < n_used
    bexp = jnp.where(used, bexp, bexp[jnp.maximum(n_used - 1, 0)])
    j = (blk * bm - pad_start[bexp])[:, None] + jnp.arange(bm, dtype=jnp.int32)[None, :]
    cnt = counts[bexp][:, None]
    valid = (j < cnt) & used[:, None]
    a = order[jnp.clip(start[bexp][:, None] + j, 0, n_assign - 1)]
    slot_tok = jnp.where(valid, a // TOP_K, 0)
    pads_before = (pad_start - start)[bexp][:, None]
    pad_dst = jnp.where(used[:, None], n_assign + pads_before + (j - cnt),
                        (blk * bm)[:, None] + jnp.arange(bm, dtype=jnp.int32)[None, :])
    slot_dst = jnp.where(valid, (a % TOP_K) * n_tok + a // TOP_K, pad_dst)
    return (bexp.astype(jnp.int32), n_used.reshape(1), slot_tok.astype(jnp.int32).reshape(nb, 1, bm),
            slot_dst.astype(jnp.int32).reshape(nb, 1, bm))


def _expert_kernel(bexp_ref, nused_ref, tok0_ref, tokn_ref, dstp_ref, x_hbm, w1_ref, w3_ref, w2_ref,
                   out_hbm, xbuf, ybuf, zbuf, w13, w2b, gsem, ssem, zsem, *, bm, nb):
    i = pl.program_id(0)
    n_used = nused_ref[0]
    slot = i % 2
    half = bm // 2

    def gather_rows(tok_ref, s, r0, r1):
        for r in range(r0, r1):
            pltpu.make_async_copy(x_hbm.at[tok_ref[0, 0, r]], xbuf.at[s, r], gsem.at[s]).start()

    def scatter_rows(s, r0, r1):
        for r in range(r0, r1):
            pltpu.make_async_copy(ybuf.at[s, r], out_hbm.at[dstp_ref[0, 0, r]], ssem.at[s]).start()

    def wait_gather(s):
        pltpu.make_async_copy(xbuf.at[s], xbuf.at[s], gsem.at[s]).wait()

    def wait_scatter(s):
        pltpu.make_async_copy(ybuf.at[s], ybuf.at[s], ssem.at[s]).wait()

    @pl.when(i == 0)
    def _():
        zbuf[...] = jnp.zeros_like(zbuf)
        ybuf[1] = jnp.zeros(ybuf.shape[1:], F32)
        gather_rows(tok0_ref, 0, 0, bm)

    @pl.when(i < n_used)
    def _():
        changed = jnp.logical_or(i == 0, bexp_ref[i] != bexp_ref[jnp.maximum(i - 1, 0)])

        @pl.when(changed)
        def _():
            w13[:, :D_EXPERT] = w1_ref[...].astype(BF16)
            w13[:, D_EXPERT:] = w3_ref[...].astype(BF16)
            w2b[...] = w2_ref[...].astype(BF16)

    def block_step(s):
        wait_gather(s)
        xb = _from_token_rows(xbuf[s]).astype(BF16)
        gate = jnp.dot(xb, w13[:, :D_EXPERT], preferred_element_type=F32)
        gather_rows(tokn_ref, 1 - s, 0, half)
        up = jnp.dot(xb, w13[:, D_EXPERT:], preferred_element_type=F32)
        gather_rows(tokn_ref, 1 - s, half, bm)
        act = (gate * (1.0 / (1.0 + jnp.exp(-gate))) * up).astype(BF16)
        y_lo = jnp.dot(act, w2b[:, :D_MODEL // 2], preferred_element_type=F32)
        scatter_rows(1 - s, 0, half)
        y_hi = jnp.dot(act, w2b[:, D_MODEL // 2:], preferred_element_type=F32)
        scatter_rows(1 - s, half, bm)

        @pl.when(i >= 1)
        def _():
            wait_scatter(s)

        ybuf[s] = _to_token_rows(jnp.concatenate([y_lo, y_hi], axis=1))

    for s in range(2):
        pl.when(jnp.logical_and(i < n_used, slot == s))(functools.partial(block_step, s))

    @pl.when(i == n_used)
    def _():
        wait_gather(slot)
        scatter_rows(1 - slot, 0, bm)
        wait_scatter(0)
        wait_scatter(1)

    @pl.when(jnp.logical_and(i >= n_used, i < nb))
    def _():
        cp = pltpu.make_async_copy(zbuf, out_hbm.at[pl.ds(i * bm, bm)], zsem)
        cp.start()
        cp.wait()


def _experts(x1, plan, w_gate, w_up, w_down, layer):
    bexp, n_used, slot_tok, slot_dst = plan
    nb, _, bm = slot_tok.shape
    spare = (nb * bm + jnp.arange(bm, dtype=jnp.int32)).reshape(1, 1, bm)
    dst_ext = jnp.concatenate([slot_dst, spare], axis=0)
    smem_blk = lambda fn: pl.BlockSpec((1, 1, bm), fn, memory_space=pltpu.SMEM)
    wspec = lambda k, m: pl.BlockSpec((None, None, k, m),
                                      lambda i, be, nu: (layer, be[jnp.minimum(i, nb - 1)], 0, 0))
    grid_spec = pltpu.PrefetchScalarGridSpec(
        num_scalar_prefetch=2,
        grid=(nb + 1,),
        in_specs=[smem_blk(lambda i, be, nu: (0, 0, 0)),
                  smem_blk(lambda i, be, nu: (jnp.minimum(i + 1, nb - 1), 0, 0)),
                  smem_blk(lambda i, be, nu: (jnp.where(i == 0, nb, i - 1), 0, 0)),
                  pl.BlockSpec(memory_space=pl.ANY),
                  wspec(D_MODEL, D_EXPERT), wspec(D_MODEL, D_EXPERT), wspec(D_EXPERT, D_MODEL)],
        out_specs=pl.BlockSpec(memory_space=pl.ANY),
        scratch_shapes=[pltpu.VMEM((2, bm, ROW_TILES, LANES), F32), pltpu.VMEM((2, bm, ROW_TILES, LANES), F32),
                        pltpu.VMEM((bm, ROW_TILES, LANES), F32),
                        pltpu.VMEM((D_MODEL, 2 * D_EXPERT), BF16), pltpu.VMEM((D_EXPERT, D_MODEL), BF16),
                        pltpu.SemaphoreType.DMA((2,)), pltpu.SemaphoreType.DMA((2,)), pltpu.SemaphoreType.DMA(())],
    )
    return pl.pallas_call(
        functools.partial(_expert_kernel, bm=bm, nb=nb),
        grid_spec=grid_spec,
        out_shape=jax.ShapeDtypeStruct(((nb + 1) * bm, ROW_TILES, LANES), F32),
        compiler_params=_cparams(("arbitrary",)),
        name="experts",
    )(bexp, n_used, slot_tok, slot_tok, dst_ext, x1, w_gate, w_up, w_down)


def _post_moe_kernel(x1_ref, y0_ref, y1_ref, route_ref, g_ref, b_ref, o_ref):
    r = route_ref[...]
    f = _from_token_rows(y0_ref[...]) * r[:, 2:3] + _from_token_rows(y1_ref[...]) * r[:, 3:4]
    o_ref[...] = _layer_norm(DN_ALPHA * _from_token_rows(x1_ref[...]) + f, g_ref[...], b_ref[...])


def _post_moe(x1, y, route, g, b):
    n = x1.shape[0]
    tm = TOKEN_TILE
    nt = n // tm
    full = lambda arr: pl.BlockSpec(arr.shape, lambda i: (0,) * arr.ndim)
    rows3 = lambda fn: pl.BlockSpec((tm, ROW_TILES, LANES), fn)
    return pl.pallas_call(
        _post_moe_kernel,
        grid=(nt,),
        in_specs=[rows3(lambda i: (i, 0, 0)),
                  rows3(lambda i: (i, 0, 0)),
                  rows3(lambda i: (nt + i, 0, 0)),
                  pl.BlockSpec((tm, ROUTE_LANES), lambda i: (i, 0)),
                  full(g), full(b)],
        out_specs=pl.BlockSpec((tm, D_MODEL), lambda i: (i, 0)),
        out_shape=jax.ShapeDtypeStruct((n, D_MODEL), F32),
        compiler_params=_cparams(("parallel",)),
        name="post_moe",
    )(x1, y, y, route, g, b)


def _prep_even(w_in, w_uq, w_ukv, w_o):
    c0 = 3 * A_W + B_Q_RANK + B_KV_RANK
    win = jnp.zeros((D_MODEL, EVEN_IN_PAD), F32)
    win = win.at[:, :c0].set(w_in[:, :c0])
    kr0 = EVEN_IN_PAD - LANES + KR_LANE0
    win = win.at[:, kr0:kr0 + B_ROPE].set(w_in[:, c0:c0 + B_ROPE])
    wuq = w_uq.reshape(B_Q_RANK, B_HEADS, B_NOPE + B_ROPE)
    wuq = jnp.pad(wuq, ((0, 0), (0, 0), (0, LANES - B_NOPE - B_ROPE))).reshape(B_Q_RANK, B_HEADS * LANES)
    wukv = w_ukv.reshape(B_KV_RANK, B_HEADS, B_NOPE + B_V)
    wk = jnp.pad(wukv[:, :, :B_NOPE], ((0, 0), (0, 0), (0, LANES - B_NOPE))).reshape(B_KV_RANK, B_HEADS * LANES)
    wv = wukv[:, :, B_NOPE:].reshape(B_KV_RANK, B_HEADS * B_V)
    return win.astype(BF16), wuq.astype(BF16), wk.astype(BF16), wv.astype(BF16), w_o.astype(BF16)


def _prep_router(w_rg, b_rg, w_re, b_re):
    wr = jnp.zeros((D_MODEL, ROUTE_LANES), F32)
    wr = wr.at[:, :N_GROUPS].set(w_rg).at[:, N_GROUPS:N_GROUPS + N_EXPERTS].set(w_re)
    br = jnp.zeros((1, ROUTE_LANES), F32)
    br = br.at[0, :N_GROUPS].set(b_rg).at[0, N_GROUPS:N_GROUPS + N_EXPERTS].set(b_re)
    wr_hi = wr.astype(BF16)
    wr_lo = (wr - wr_hi.astype(F32)).astype(BF16)
    return jnp.concatenate([wr_hi, wr_lo], axis=1), br


def kernel(x, even_w_in, even_lam_q1, even_lam_k1, even_lam_q2, even_lam_k2, even_subln_g, even_q_norm_g, even_w_uq, even_kv_norm_g, even_w_ukv, even_w_o, odd_w_qkv, odd_rel_bias, odd_w_o, ln1_g, ln1_b, ln2_g, ln2_b, w_router_group, b_router_group, w_router_expert, b_router_expert, w_gate, w_up, w_down):
    bsz, seq, d = x.shape
    n = bsz * seq
    rows = seq // GRID_W
    x2d = x.reshape(n, d)

    a_scale = A_HEAD_DIM ** -0.5 * LOG2E
    b_scale = (B_NOPE + B_ROPE) ** -0.5 * LOG2E
    a_lanes = (0, A_HEAD_DIM)
    rope_tabs = jnp.concatenate([
        _rope_block_tables(seq, A_ROT, a_lanes, a_scale),
        _rope_block_tables(seq, A_ROT, a_lanes, 1.0),
        _rope_block_tables(seq, B_ROPE, (B_NOPE,), b_scale),
        _rope_block_tables(seq, B_ROPE, (KR_LANE0,), 1.0)], axis=0)

    for i in range(DEPTH):
        j = i // 2
        if i % 2 == 0:
            lambda_init = 0.8 - 0.6 * math.exp(-0.3 * i)
            win, wuq, wk, wv, w_o = _prep_even(even_w_in[j], even_w_uq[j], even_w_ukv[j], even_w_o[j])
            lam = (jnp.exp(jnp.sum(even_lam_q1[j] * even_lam_k1[j]))
                   - jnp.exp(jnp.sum(even_lam_q2[j] * even_lam_k2[j])) + lambda_init).reshape(1).astype(F32)
            qkva, qb, kb, vb = _even_proj(x2d, win, wuq, wk, wv, even_q_norm_g[j].reshape(1, -1),
                                          even_kv_norm_g[j].reshape(1, -1), rope_tabs, seq)
            by_seq = lambda t: t.reshape(t.shape[0], bsz, seq, LANES)
            a_out = _diff_attn(by_seq(qkva), lam, even_subln_g[j].reshape(1, -1), 1.0 - lambda_init)
            b_out = _mla_attn(by_seq(qb), by_seq(kb), by_seq(vb))
            acts = [a_out.reshape(-1, n, LANES), b_out.reshape(-1, n, LANES)]
        else:
            qkv = _proj(x2d, odd_w_qkv[j].astype(BF16), C_WIDTH, C_HEAD_DIM ** -0.5 * LOG2E)
            o = _natten(qkv.reshape(-1, bsz, seq, LANES), _natten_tables(odd_rel_bias[j] * LOG2E, rows))
            acts = [o.reshape(-1, n, LANES)]
            w_o = odd_w_o[j].astype(BF16)
        wr, br = _prep_router(w_router_group[i], b_router_group[i], w_router_expert[i], b_router_expert[i])
        x1, route = _post_attn(acts, w_o, x2d, ln1_g[i].reshape(1, -1), ln1_b[i].reshape(1, -1), wr, br)
        plan = _moe_plan(route, n, MOE_BLOCK)
        y = _experts(x1, plan, w_gate, w_up, w_down, i)
        x2d = _post_moe(x1, y, route,ln2_g[i].reshape(1, -1), ln2_b[i].reshape(1, -1))
    return x2d.reshape(bsz, seq, d)
```
